```python
import jax, jax.numpy as jnp
from jax import lax
import numpy as np

D_MODEL = 1024
BATCH = 2
SEQ = 8192
DEPTH = 1
DEC_BATCH = 128
DEC_SEQ = 8
PAST_LEN = 8192
PAGE_SIZE = 128

N_HEADS = 8
HEAD_DIM = 64
ATTN_WIDTH = N_HEADS * HEAD_DIM
LRU_WIDTH = D_MODEL
LRU_BLOCKS = 8
LRU_BLOCK = LRU_WIDTH // LRU_BLOCKS
CONV_WIDTH = 4
LRU_C = 8.0
Q_BLOCK = 128
NORM_EPS = 1e-6
ATTN_SCALE = HEAD_DIM ** -0.5
IN_SPLITS = (ATTN_WIDTH, ATTN_WIDTH, ATTN_WIDTH, ATTN_WIDTH, N_HEADS, LRU_WIDTH, LRU_WIDTH, D_MODEL, D_MODEL)
IN_WIDTH = 4 * ATTN_WIDTH + N_HEADS + 2 * LRU_WIDTH + 2 * D_MODEL

kernel_name = 'fox_rglru_hybrid_step'


def _rms(x, gain):
    xf = x.astype(jnp.float32)
    y = xf * lax.rsqrt(jnp.mean(xf * xf, axis=-1, keepdims=True) + NORM_EPS)
    return (y * gain.astype(jnp.float32)).astype(x.dtype)


def _split_in(u):
    out, idx = [], 0
    for n in IN_SPLITS:
        out.append(u[..., idx:idx + n])
        idx += n
    return out


def _mixer_inputs(x, norm_gain, w_in, q_gain, k_gain, b_forget):
    n, t, _ = x.shape
    u = _rms(x, norm_gain) @ w_in
    q, k, v, z_a, f_logit, x_l, z_l, g_a, g_b = _split_in(u)
    q = _rms(q.reshape(n, t, N_HEADS, HEAD_DIM), q_gain)
    k = _rms(k.reshape(n, t, N_HEADS, HEAD_DIM), k_gain)
    v = v.reshape(n, t, N_HEADS, HEAD_DIM)
    logf = jax.nn.log_sigmoid(f_logit.astype(jnp.float32) + b_forget.astype(jnp.float32))
    return q, k, v, logf, z_a, x_l, z_l, g_a, g_b


def _fox_prompt(q, k, v, logf):
    b, t = q.shape[0], q.shape[1]
    c = jnp.swapaxes(jnp.cumsum(logf, axis=1), 1, 2)
    kf = k.astype(jnp.float32)
    vf = v.astype(jnp.float32)
    pos_k = jnp.arange(t)

    def block(i):
        s0 = i * Q_BLOCK
        qb = lax.dynamic_slice_in_dim(q, s0, Q_BLOCK, 1).astype(jnp.float32) * ATTN_SCALE
        cb = lax.dynamic_slice_in_dim(c, s0, Q_BLOCK, 2)
        logits = jnp.einsum('bqhd,bkhd->bhqk', qb, kf) + cb[..., :, None] - c[:, :, None, :]
        pos_q = s0 + jnp.arange(Q_BLOCK)
        mask = pos_k[None, :] <= pos_q[:, None]
        p = jax.nn.softmax(jnp.where(mask, logits, -jnp.inf), axis=-1)
        return jnp.einsum('bhqk,bkhd->bqhd', p, vf)

    out = lax.map(block, jnp.arange(t // Q_BLOCK))
    return jnp.moveaxis(out, 0, 1).reshape(b, t, ATTN_WIDTH)


def _fox_sample(q, k, v, logf, cache_k, cache_v, cache_logf, page_table, layer):
    db, s = q.shape[0], q.shape[1]
    n_pages = page_table.shape[1]
    past_logf = cache_logf[layer, page_table].astype(jnp.float32).reshape(db, n_pages * PAGE_SIZE, N_HEADS)
    c_past = jnp.cumsum(past_logf, axis=1)
    c_new = c_past[:, -1:, :] + jnp.cumsum(logf, axis=1)
    cq = jnp.swapaxes(c_new, 1, 2)
    qf = q.astype(jnp.float32) * ATTN_SCALE
    kf = k.astype(jnp.float32)
    vf = v.astype(jnp.float32)
    s_new = jnp.einsum('bqhd,bkhd->bhqk', qf, kf) + cq[..., :, None] - cq[..., None, :]
    mask = jnp.tril(jnp.ones((s, s), dtype=bool))
    s_new = jnp.where(mask, s_new, -jnp.inf)
    m0 = jnp.max(s_new, axis=-1)
    p0 = jnp.exp(s_new - m0[..., None])
    l0 = jnp.sum(p0, axis=-1)
    acc0 = jnp.einsum('bhqk,bkhd->bhqd', p0, vf)
    c_pages = jnp.moveaxis(c_past.reshape(db, n_pages, PAGE_SIZE, N_HEADS), 1, 0)

    def step(carry, xs):
        m, l, acc = carry
        pt, cp = xs
        kp = cache_k[layer, pt].astype(jnp.float32)
        vp = cache_v[layer, pt].astype(jnp.float32)
        sc = jnp.einsum('bqhd,bkhd->bhqk', qf, kp) + cq[..., :, None] - jnp.swapaxes(cp, 1, 2)[:, :, None, :]
        m_new = jnp.maximum(m, jnp.max(sc, axis=-1))
        alpha = jnp.exp(m - m_new)
        p = jnp.exp(sc - m_new[..., None])
        l = l * alpha + jnp.sum(p, axis=-1)
        acc = acc * alpha[..., None] + jnp.einsum('bhqk,bkhd->bhqd', p, vp)
        return (m_new, l, acc), None

    (m, l, acc), _ = lax.scan(step, (m0, l0, acc0), (page_table.T, c_pages))
    out = acc / l[..., None]
    return jnp.transpose(out, (0, 2, 1, 3)).reshape(db, s, ATTN_WIDTH)


def _lru_branch(x_l, conv_state, h0, conv_w, conv_b, w_r, b_r, w_i, b_i, lam):
    n, t, c = x_l.shape
    xp = jnp.concatenate([conv_state.astype(x_l.dtype), x_l], axis=1)
    xc = conv_b + sum(xp[:, j:j + t] * conv_w[j] for j in range(CONV_WIDTH))
    new_conv = xp[:, -(CONV_WIDTH - 1):]
    xf = xc.astype(jnp.float32)
    xb = xf.reshape(n, t, LRU_BLOCKS, LRU_BLOCK)
    r = jax.nn.sigmoid(jnp.einsum('ntgi,gij->ntgj', xb, w_r.astype(jnp.float32)).reshape(n, t, c) + b_r.astype(jnp.float32))
    i = jax.nn.sigmoid(jnp.einsum('ntgi,gij->ntgj', xb, w_i.astype(jnp.float32)).reshape(n, t, c) + b_i.astype(jnp.float32))
    log_a = LRU_C * r * jax.nn.log_sigmoid(lam.astype(jnp.float32))
    a = jnp.exp(log_a)
    bx = jnp.sqrt(-jnp.expm1(2.0 * log_a)) * (i * xf)

    def combine(left, right):
        a1, b1 = left
        a2, b2 = right
        return a1 * a2, a2 * b1 + b2

    a_cum, b_cum = lax.associative_scan(combine, (a, bx), axis=1)
    h = a_cum * h0.astype(jnp.float32)[:, None, :] + b_cum
    return h, new_conv, h[:, -1]


def _merge(x, o_attn, o_lru, z_a, z_l, g_a, g_b, w_pa, w_pl, w_out):
    dt = x.dtype
    a = (o_attn * jax.nn.silu(z_a.astype(jnp.float32))).astype(dt) @ w_pa
    r = (o_lru * jax.nn.silu(z_l.astype(jnp.float32))).astype(dt) @ w_pl
    mixed = jax.nn.sigmoid(g_a) * a + jax.nn.sigmoid(g_b) * r
    return x + mixed @ w_out


def setup_inputs(seed: int = 0) -> dict:
    key = jax.random.key(seed)
    ks = jax.random.split(key, 24)
    n_pages = PAST_LEN // PAGE_SIZE
    used = DEC_BATCH * n_pages
    n_pool = used + (used + 3) // 4
    f32 = jnp.float32
    page_table = jax.random.permutation(ks[0], n_pool)[:used].reshape(DEC_BATCH, n_pages).astype(jnp.int32)
    a0 = jax.random.uniform(ks[1], (DEPTH, LRU_WIDTH), f32, 0.9, 0.999)
    s_l = a0 ** (1.0 / LRU_C)
    lru_lambda = jnp.log(s_l) - jnp.log1p(-s_l)
    return {
        'x_prompt': jax.random.normal(ks[2], (BATCH, SEQ, D_MODEL), f32),
        'x_sample': jax.random.normal(ks[3], (DEC_BATCH, DEC_SEQ, D_MODEL), f32),
        'cache_k': jax.random.normal(ks[4], (DEPTH, n_pool, PAGE_SIZE, N_HEADS, HEAD_DIM), f32),
        'cache_v': jax.random.normal(ks[5], (DEPTH, n_pool, PAGE_SIZE, N_HEADS, HEAD_DIM), f32),
        'cache_logf': jax.nn.log_sigmoid(jax.random.uniform(ks[6], (DEPTH, n_pool, PAGE_SIZE, N_HEADS), f32, 1.0, 4.0) + jax.random.normal(ks[7], (DEPTH, n_pool, PAGE_SIZE, N_HEADS), f32)),
        'state_conv': jax.random.normal(ks[8], (DEPTH, DEC_BATCH, CONV_WIDTH - 1, LRU_WIDTH), f32),
        'state_h': 0.5 * jax.random.normal(ks[9], (DEPTH, DEC_BATCH, LRU_WIDTH), f32),
        'page_table': page_table,
        'norm_gain': 1.0 + 0.02 * jax.random.normal(ks[10], (DEPTH, D_MODEL), f32),
        'w_in': jax.random.normal(ks[11], (DEPTH, D_MODEL, IN_WIDTH), f32) * D_MODEL ** -0.5,
        'q_norm_gain': 1.0 + 0.02 * jax.random.normal(ks[12], (DEPTH, HEAD_DIM), f32),
        'k_norm_gain': 1.0 + 0.02 * jax.random.normal(ks[13], (DEPTH, HEAD_DIM), f32),
        'b_forget': jax.random.uniform(ks[14], (DEPTH, N_HEADS), f32, 1.0, 4.0),
        'conv_w': jax.random.normal(ks[15], (DEPTH, CONV_WIDTH, LRU_WIDTH), f32) * CONV_WIDTH ** -0.5,
        'conv_b': 0.01 * jax.random.normal(ks[16], (DEPTH, LRU_WIDTH), f32),
        'w_rec_gate': jax.random.normal(ks[17], (DEPTH, LRU_BLOCKS, LRU_BLOCK, LRU_BLOCK), f32) * LRU_BLOCK ** -0.5,
        'b_rec_gate': 0.01 * jax.random.normal(ks[18], (DEPTH, LRU_WIDTH), f32),
        'w_in_gate': jax.random.normal(ks[19], (DEPTH, LRU_BLOCKS, LRU_BLOCK, LRU_BLOCK), f32) * LRU_BLOCK ** -0.5,
        'b_in_gate': 0.01 * jax.random.normal(ks[20], (DEPTH, LRU_WIDTH), f32),
        'lru_lambda': lru_lambda,
        'w_proj_attn': jax.random.normal(ks[21], (DEPTH, ATTN_WIDTH, D_MODEL), f32) * ATTN_WIDTH ** -0.5,
        'w_proj_lru': jax.random.normal(ks[22], (DEPTH, LRU_WIDTH, D_MODEL), f32) * LRU_WIDTH ** -0.5,
        'w_out': jax.random.normal(ks[23], (DEPTH, D_MODEL, D_MODEL), f32) * D_MODEL ** -0.5,
    }


def reference(x_prompt, x_sample, cache_k, cache_v, cache_logf, state_conv, state_h, page_table,
              norm_gain, w_in, q_norm_gain, k_norm_gain, b_forget, conv_w, conv_b,
              w_rec_gate, b_rec_gate, w_in_gate, b_in_gate, lru_lambda,
              w_proj_attn, w_proj_lru, w_out):
    hp, hs = x_prompt, x_sample
    kp_l, vp_l, fp_l, cp_l, hp_l = [], [], [], [], []
    ks_l, vs_l, fs_l, cs_l, hs_l = [], [], [], [], []
    for l in range(DEPTH):
        lru_w = (conv_w[l], conv_b[l], w_rec_gate[l], b_rec_gate[l], w_in_gate[l], b_in_gate[l], lru_lambda[l])
        out_w = (w_proj_attn[l], w_proj_lru[l], w_out[l])
        q, k, v, logf, z_a, x_l, z_l, g_a, g_b = _mixer_inputs(hp, norm_gain[l], w_in[l], q_norm_gain[l], k_norm_gain[l], b_forget[l])
        o_attn = _fox_prompt(q, k, v, logf)
        conv0 = jnp.zeros((hp.shape[0], CONV_WIDTH - 1, LRU_WIDTH), x_l.dtype)
        h0 = jnp.zeros((hp.shape[0], LRU_WIDTH), jnp.float32)
        o_lru, new_conv, h_last = _lru_branch(x_l, conv0, h0, *lru_w)
        hp = _merge(hp, o_attn, o_lru, z_a, z_l, g_a, g_b, *out_w)
        kp_l.append(k); vp_l.append(v); fp_l.append(logf); cp_l.append(new_conv); hp_l.append(h_last)
        q, k, v, logf, z_a, x_l, z_l, g_a, g_b = _mixer_inputs(hs, norm_gain[l], w_in[l], q_norm_gain[l], k_norm_gain[l], b_forget[l])
        o_attn = _fox_sample(q, k, v, logf, cache_k, cache_v, cache_logf, page_table, l)
        o_lru, new_conv, h_last = _lru_branch(x_l, state_conv[l], state_h[l], *lru_w)
        hs = _merge(hs, o_attn, o_lru, z_a, z_l, g_a, g_b, *out_w)
        ks_l.append(k); vs_l.append(v); fs_l.append(logf); cs_l.append(new_conv); hs_l.append(h_last)
    return (hp, hs,
            jnp.stack(kp_l), jnp.stack(vp_l), jnp.stack(fp_l), jnp.stack(cp_l), jnp.stack(hp_l),
            jnp.stack(ks_l), jnp.stack(vs_l), jnp.stack(fs_l), jnp.stack(cs_l), jnp.stack(hs_l))
```

```python
import functools

import jax
import jax.numpy as jnp
from jax import lax
from jax.experimental import pallas as pl
from jax.experimental.pallas import tpu as pltpu

F32 = jnp.float32
BF16 = jnp.bfloat16

D_MODEL = 1024
N_HEADS = 8
HEAD_DIM = 64
ATTN_WIDTH = N_HEADS * HEAD_DIM
LRU_WIDTH = D_MODEL
LRU_BLOCKS = 8
LRU_BLOCK = LRU_WIDTH // LRU_BLOCKS
CONV_WIDTH = 4
LRU_C = 8.0
NORM_EPS = 1e-6
ATTN_SCALE = HEAD_DIM ** -0.5
PAGE_SIZE = 128

V7X_LANES = 128
V7X_SUBLANES = 8
V7X_VMEM_LIMIT_BYTES = 56 * 1024 * 1024

HEAD_SLOT = 2 * HEAD_DIM
AUG_ONES = 3

NT_DIMS = (((1,), (1,)), ((), ()))


def _compiler_params(semantics):
    return pltpu.CompilerParams(dimension_semantics=semantics,
                                vmem_limit_bytes=V7X_VMEM_LIMIT_BYTES)


def _const_spec(shape):
    zeros = (0,) * len(shape)
    return pl.BlockSpec(shape, lambda *_: zeros, pipeline_mode=pl.Buffered(1))


def _split3(x):
    hi = x.astype(BF16).astype(F32)
    r = x - hi
    mid = r.astype(BF16).astype(F32)
    lo = (r - mid).astype(BF16).astype(F32)
    return hi, mid, lo


def _exact_dot(x, w):
    m = x.shape[0]
    parts = jnp.concatenate(_split3(x), axis=0).astype(BF16)
    y = jnp.dot(parts, w, preferred_element_type=F32)
    return y[:m] + y[m:2 * m] + y[2 * m:]


def _log_sigmoid(x):
    return jnp.minimum(x, 0.0) - jnp.log1p(jnp.exp(-jnp.abs(x)))


def _sigmoid(x):
    return jax.nn.sigmoid(x)


def _silu(x):
    return x * jax.nn.sigmoid(x)


def _rms_rows(x, gain_row):
    ms = jnp.mean(x * x, axis=-1, keepdims=True)
    return x * lax.rsqrt(ms + NORM_EPS) * gain_row


def _head_rms_rowmajor(u, gain_row, blockdiag_ones):
    ssq = jnp.dot((u * u).astype(BF16), blockdiag_ones, preferred_element_type=F32)
    return u * lax.rsqrt(ssq * (1.0 / HEAD_DIM) + NORM_EPS) * gain_row


def _inproj_prompt_kernel(x_ref, ng_ref, wq_ref, wk_ref, wv_ref, wza_ref, wf_ref,
                          wxl_ref, wzl_ref, wga_ref, wgb_ref,
                          qg_ref, kg_ref, kgrow_ref, bf_ref, bd_ref, tri_ref,
                          qt_ref, kp_ref, kt_ref, vt_ref, vtb_ref, szat_ref, lft_ref,
                          xl_ref, szl_ref, sga_ref, sgb_ref,
                          carry_ref):
    tm = x_ref.shape[1]

    @pl.when(pl.program_id(1) == 0)
    def _():
        carry_ref[...] = jnp.zeros_like(carry_ref)

    h = _rms_rows(x_ref[0], ng_ref[...]).astype(BF16)

    q_t = lax.dot_general(wq_ref[...], h, NT_DIMS, preferred_element_type=F32)
    k_t = lax.dot_general(wk_ref[...], h, NT_DIMS, preferred_element_type=F32)
    v_t = lax.dot_general(wv_ref[...], h, NT_DIMS, preferred_element_type=F32)
    za_t = lax.dot_general(wza_ref[...], h, NT_DIMS, preferred_element_type=F32)
    f_t = lax.dot_general(wf_ref[...], h, NT_DIMS, preferred_element_type=F32)

    logf_t = _log_sigmoid(f_t + bf_ref[...])
    lft_ref[0] = logf_t
    c_t = _exact_dot(logf_t, tri_ref[...]) + carry_ref[:, 0:1]
    carry_ref[...] = jnp.broadcast_to(c_t[:, tm - 1:tm], carry_ref.shape)
    c_hi, c_mid, c_lo = _split3(c_t)

    vt_ref[0] = v_t
    vtb_ref[0] = v_t.astype(BF16)
    szat_ref[0] = _silu(za_t).astype(BF16)

    row8 = lax.broadcasted_iota(jnp.int32, (V7X_SUBLANES, tm), 0)
    ones_rows = jnp.where(row8 < AUG_ONES, 1.0, 0.0).astype(F32)
    zeros32 = jnp.zeros((HEAD_DIM - 4 * V7X_SUBLANES, tm), F32)
    for hd in range(N_HEADS):
        rows = slice(hd * HEAD_DIM, (hd + 1) * HEAD_DIM)
        qh = q_t[rows]
        qn = qh * lax.rsqrt(jnp.mean(qh * qh, axis=0, keepdims=True) + NORM_EPS) * qg_ref[...]
        qn = qn * ATTN_SCALE
        kh = k_t[rows]
        kn = kh * lax.rsqrt(jnp.mean(kh * kh, axis=0, keepdims=True) + NORM_EPS) * kg_ref[...]
        kt_ref[0, rows, :] = kn
        sel = jnp.where(row8 == hd, -1.0, 0.0).astype(F32)
        cq = jnp.where(row8 == 0, c_hi[hd:hd + 1],
                       jnp.where(row8 == 1, c_mid[hd:hd + 1],
                                 jnp.where(row8 == 2, c_lo[hd:hd + 1], 0.0)))
        aug = jnp.concatenate([cq, sel, sel, sel, zeros32], axis=0)
        if hd % 2 == 0:
            slot = jnp.concatenate([qn, aug], axis=0)
        else:
            slot = jnp.concatenate([aug, qn], axis=0)
        qt_ref[0, hd] = slot.astype(BF16)

    aug_k_t = jnp.concatenate([ones_rows, c_hi, c_mid, c_lo, zeros32], axis=0)
    aug_k = jnp.transpose(jnp.concatenate([aug_k_t, aug_k_t], axis=0))

    k_row = lax.dot_general(h, wk_ref[...], NT_DIMS, preferred_element_type=F32)
    k_row = _head_rms_rowmajor(k_row, kgrow_ref[...], bd_ref[...])
    lane = lax.broadcasted_iota(jnp.int32, (tm, HEAD_SLOT), 1)
    for pair in range(N_HEADS // 2):
        kcol = k_row[:, pair * HEAD_SLOT:(pair + 1) * HEAD_SLOT]
        even = jnp.where(lane < HEAD_DIM, kcol, aug_k)
        odd = jnp.where(lane < HEAD_DIM, aug_k, kcol)
        kp_ref[0, :, (2 * pair) * HEAD_SLOT:(2 * pair + 1) * HEAD_SLOT] = even.astype(BF16)
        kp_ref[0, :, (2 * pair + 1) * HEAD_SLOT:(2 * pair + 2) * HEAD_SLOT] = odd.astype(BF16)

    xl_ref[0] = lax.dot_general(h, wxl_ref[...], NT_DIMS, preferred_element_type=F32)
    szl_ref[0] = _silu(lax.dot_general(h, wzl_ref[...], NT_DIMS,
                                       preferred_element_type=F32)).astype(BF16)
    sga_ref[0] = _sigmoid(lax.dot_general(h, wga_ref[...], NT_DIMS,
                                          preferred_element_type=F32)).astype(BF16)
    sgb_ref[0] = _sigmoid(lax.dot_general(h, wgb_ref[...], NT_DIMS,
                                          preferred_element_type=F32)).astype(BF16)


def _inproj_prompt(x, ng, w, qg_col, kg_col, kg_row, bf_col, bd, tm):
    b, t, _ = x.shape
    tri = jnp.triu(jnp.ones((tm, tm), F32)).astype(BF16)
    row_spec = lambda n: pl.BlockSpec((1, tm, n), lambda i, j: (i, j, 0))
    col_spec = lambda n: pl.BlockSpec((1, n, tm), lambda i, j: (i, 0, j))
    out_shape = (
        jax.ShapeDtypeStruct((b, N_HEADS, HEAD_SLOT, t), BF16),
        jax.ShapeDtypeStruct((b, t, N_HEADS * HEAD_SLOT), BF16),
        jax.ShapeDtypeStruct((b, ATTN_WIDTH, t), F32),
        jax.ShapeDtypeStruct((b, ATTN_WIDTH, t), F32),
        jax.ShapeDtypeStruct((b, ATTN_WIDTH, t), BF16),
        jax.ShapeDtypeStruct((b, ATTN_WIDTH, t), BF16),
        jax.ShapeDtypeStruct((b, N_HEADS, t), F32),
        jax.ShapeDtypeStruct((b, t, LRU_WIDTH), F32),
        jax.ShapeDtypeStruct((b, t, LRU_WIDTH), BF16),
        jax.ShapeDtypeStruct((b, t, D_MODEL), BF16),
        jax.ShapeDtypeStruct((b, t, D_MODEL), BF16),
    )
    out_specs = (
        pl.BlockSpec((1, N_HEADS, HEAD_SLOT, tm), lambda i, j: (i, 0, 0, j)),
        row_spec(N_HEADS * HEAD_SLOT),
        col_spec(ATTN_WIDTH), col_spec(ATTN_WIDTH), col_spec(ATTN_WIDTH), col_spec(ATTN_WIDTH),
        col_spec(N_HEADS),
        row_spec(LRU_WIDTH), row_spec(LRU_WIDTH), row_spec(D_MODEL), row_spec(D_MODEL),
    )
    consts = (ng, w["q"], w["k"], w["v"], w["za"], w["f"], w["xl"], w["zl"], w["ga"], w["gb"],
              qg_col, kg_col, kg_row, bf_col, bd, tri)
    return pl.pallas_call(
        _inproj_prompt_kernel,
        grid=(b, t // tm),
        in_specs=[row_spec(D_MODEL)] + [_const_spec(c.shape) for c in consts],
        out_specs=out_specs,
        out_shape=out_shape,
        scratch_shapes=[pltpu.VMEM((N_HEADS, V7X_LANES), F32)],
        compiler_params=_compiler_params(("arbitrary", "arbitrary")),
        name="inproj_prompt",
    )(x, *consts)


def _lru_gates(xc, wri_ref, br_row, bi_row, log_sig_lam_row):
    xb = xc.astype(BF16)
    pre_r, pre_i = [], []
    for g in range(LRU_BLOCKS):
        cols = slice(g * LRU_BLOCK, (g + 1) * LRU_BLOCK)
        ri = jnp.dot(xb[:, cols], wri_ref[g], preferred_element_type=F32)
        pre_r.append(ri[:, :LRU_BLOCK])
        pre_i.append(ri[:, LRU_BLOCK:])
    r = _sigmoid(jnp.concatenate(pre_r, axis=1) + br_row)
    i = _sigmoid(jnp.concatenate(pre_i, axis=1) + bi_row)
    log_a = LRU_C * r * log_sig_lam_row
    a = jnp.exp(log_a)
    b = jnp.sqrt(-jnp.tanh(log_a) * (1.0 + a * a)) * (i * xc)
    return a, b


def _lru_prompt_kernel(xl_ref, szl_ref, sgb_ref, cw_ref, cb_ref, wri_ref, br_ref, bi_ref,
                       lam_ref, wpl_ref,
                       r_ref, hlast_ref,
                       xprev_ref, hprev_ref, a_ref, b_ref, h_ref):
    tl = xl_ref.shape[1]
    nblk = tl // V7X_SUBLANES

    @pl.when(pl.program_id(1) == 0)
    def _():
        xprev_ref[...] = jnp.zeros_like(xprev_ref)
        hprev_ref[...] = jnp.zeros_like(hprev_ref)

    x = xl_ref[0]
    prev = xprev_ref[...]
    row8 = lax.broadcasted_iota(jnp.int32, (V7X_SUBLANES, LRU_WIDTH), 0)
    xc = cb_ref[...] + x * cw_ref[CONV_WIDTH - 1:CONV_WIDTH, :]
    for back in range(1, CONV_WIDTH):
        w_row = cw_ref[CONV_WIDTH - 1 - back:CONV_WIDTH - back, :]
        rolled = pltpu.roll(x, back, 0)
        head = jnp.where(row8 < back, pltpu.roll(prev, back, 0), rolled[:V7X_SUBLANES])
        shifted = jnp.concatenate([head, rolled[V7X_SUBLANES:]], axis=0)
        xc = xc + shifted * w_row
    xprev_ref[...] = x[tl - V7X_SUBLANES:]

    a, b = _lru_gates(xc, wri_ref, br_ref[...], bi_ref[...], _log_sigmoid(lam_ref[...]))
    a_ref[...] = a
    b_ref[...] = b

    def block(k, hprev):
        rows = pl.ds(pl.multiple_of(k * V7X_SUBLANES, V7X_SUBLANES), V7X_SUBLANES)
        ak = a_ref[rows, :]
        bk = b_ref[rows, :]
        for s in (1, 2, 4):
            keep = row8 >= s
            b_sh = pltpu.roll(bk, s, 0)
            a_sh = pltpu.roll(ak, s, 0)
            bk = jnp.where(keep, ak * b_sh + bk, bk)
            ak = jnp.where(keep, ak * a_sh, ak)
        hk = ak * hprev + bk
        h_ref[rows, :] = hk
        return jnp.broadcast_to(hk[V7X_SUBLANES - 1:], hk.shape)

    hlast = lax.fori_loop(0, nblk, block, hprev_ref[...])
    hprev_ref[...] = hlast
    hlast_ref[0] = hlast

    g = (h_ref[...] * szl_ref[0].astype(F32)).astype(BF16)
    r_ref[0] = sgb_ref[0].astype(F32) * jnp.dot(g, wpl_ref[...], preferred_element_type=F32)


def _lru_prompt(xl, szl, sgb, cw, cb, wri, br, bi, lam, wpl, tl):
    b, t, c = xl.shape
    row_spec = pl.BlockSpec((1, tl, c), lambda i, j: (i, j, 0))
    consts = (cw, cb, wri, br, bi, lam, wpl)
    return pl.pallas_call(
        _lru_prompt_kernel,
        grid=(b, t // tl),
        in_specs=[row_spec, row_spec, row_spec] + [_const_spec(x.shape) for x in consts],
        out_specs=(row_spec, pl.BlockSpec((1, V7X_SUBLANES, c), lambda i, j: (i, 0, 0))),
        out_shape=(jax.ShapeDtypeStruct((b, t, D_MODEL), F32),
                   jax.ShapeDtypeStruct((b, V7X_SUBLANES, c), F32)),
        scratch_shapes=[pltpu.VMEM((V7X_SUBLANES, c), F32), pltpu.VMEM((V7X_SUBLANES, c), F32),
                        pltpu.VMEM((tl, c), F32), pltpu.VMEM((tl, c), F32), pltpu.VMEM((tl, c), F32)],
        compiler_params=_compiler_params(("arbitrary", "arbitrary")),
        name="lru_prompt",
    )(xl, szl, sgb, *consts)


def _lru_sample_kernel(xl_ref, conv0_ref, h0_ref, szl_ref, sgb_ref, cw_ref, cb_ref, wri_ref,
                       br_ref, bi_ref, lam_ref, wpl_ref,
                       r_ref, hlast_ref):
    s, n, c = xl_ref.shape
    slabs = [conv0_ref[j] for j in range(CONV_WIDTH - 1)] + [xl_ref[t] for t in range(s)]
    xc = []
    for t in range(s):
        acc = cb_ref[...] + slabs[t] * cw_ref[0:1, :]
        for j in range(1, CONV_WIDTH):
            acc = acc + slabs[t + j] * cw_ref[j:j + 1, :]
        xc.append(acc)
    xc = jnp.concatenate(xc, axis=0)
    a, b = _lru_gates(xc, wri_ref, br_ref[...], bi_ref[...], _log_sigmoid(lam_ref[...]))
    h = h0_ref[...]
    hs = []
    for t in range(s):
        h = a[t * n:(t + 1) * n] * h + b[t * n:(t + 1) * n]
        hs.append(h)
    hlast_ref[...] = h
    g = (jnp.concatenate(hs, axis=0) * szl_ref[...].reshape(s * n, c).astype(F32)).astype(BF16)
    r = jnp.dot(g, wpl_ref[...], preferred_element_type=F32)
    r_ref[...] = (sgb_ref[...].reshape(s * n, c).astype(F32) * r).reshape(s, n, c)


def _lru_sample(xl, conv0, h0, szl, sgb, cw, cb, wri, br, bi, lam, wpl):
    s, n, c = xl.shape
    return pl.pallas_call(
        _lru_sample_kernel,
        out_shape=(jax.ShapeDtypeStruct((s, n, D_MODEL), F32),
                   jax.ShapeDtypeStruct((n, c), F32)),
        compiler_params=pltpu.CompilerParams(vmem_limit_bytes=V7X_VMEM_LIMIT_BYTES),
        name="lru_sample",
    )(xl, conv0, h0, szl, sgb, cw, cb, wri, br, bi, lam, wpl)


def _fox_prompt_kernel(qt_ref, kp_ref, vt_ref, ot_ref, *, tk):
    tq = qt_ref.shape[3]
    qi = pl.program_id(2)
    q_t = qt_ref[0, 0]

    def step(k_blk, v_blk, carry, mask):
        m, l, acc = carry
        s = jnp.dot(k_blk, q_t, preferred_element_type=F32)
        if mask is not None:
            s = jnp.where(mask, s, -jnp.inf)
        m_new = jnp.maximum(m, jnp.max(s, axis=0, keepdims=True))
        alpha = jnp.exp(m - m_new)
        p = jnp.exp(s - m_new)
        l = alpha * l + jnp.sum(p, axis=0, keepdims=True)
        acc = alpha * acc + jnp.dot(v_blk, p.astype(BF16), preferred_element_type=F32)
        return m_new, l, acc

    def body(j, carry):
        k0 = pl.multiple_of(j * tk, tk)
        return step(kp_ref[0, pl.ds(k0, tk), :], vt_ref[0, :, pl.ds(k0, tk)], carry, None)

    init = (jnp.full((1, tq), -jnp.inf, F32), jnp.zeros((1, tq), F32),
            jnp.zeros((HEAD_DIM, tq), F32))
    n_full = qi * (tq // tk)
    carry = lax.fori_loop(0, n_full, body, init)
    key_pos = lax.broadcasted_iota(jnp.int32, (tk, tq), 0)
    qry_pos = lax.broadcasted_iota(jnp.int32, (tk, tq), 1)
    for d in range(tq // tk):
        k0 = pl.multiple_of(qi * tq + d * tk, tk)
        carry = step(kp_ref[0, pl.ds(k0, tk), :], vt_ref[0, :, pl.ds(k0, tk)], carry,
                     key_pos + d * tk <= qry_pos)
    m, l, acc = carry
    ot_ref[0] = acc / l


def _fox_prompt(qt, kp, vtb, tq, tk):
    b, nh, _, t = qt.shape
    return pl.pallas_call(
        functools.partial(_fox_prompt_kernel, tk=tk),
        grid=(b, nh, t // tq),
        in_specs=[pl.BlockSpec((1, 1, HEAD_SLOT, tq), lambda i, h, j: (i, h, 0, j)),
                  pl.BlockSpec((1, t, HEAD_SLOT), lambda i, h, j: (i, 0, h)),
                  pl.BlockSpec((1, HEAD_DIM, t), lambda i, h, j: (i, h, 0))],
        out_specs=pl.BlockSpec((1, HEAD_DIM, tq), lambda i, h, j: (i, h, j)),
        out_shape=jax.ShapeDtypeStruct((b, ATTN_WIDTH, t), F32),
        compiler_params=_compiler_params(("arbitrary", "arbitrary", "arbitrary")),
        name="fox_prompt",
    )(qt, kp, vtb)


def _merge_kernel(x_ref, o_ref, sza_ref, sga_ref, r_ref, wpa_ref, wout_ref, y_ref, *,
                  attn_feature_major):
    g = o_ref[0] * sza_ref[0].astype(F32)
    if attn_feature_major:
        g = jnp.transpose(g)
    a = jnp.dot(g.astype(BF16), wpa_ref[...], preferred_element_type=F32)
    mixed = sga_ref[0].astype(F32) * a + r_ref[0]
    y_ref[0] = x_ref[0] + jnp.dot(mixed.astype(BF16), wout_ref[...], preferred_element_type=F32)


def _merge(x, o, sza, sga, r, wpa, wout, tm, attn_feature_major):
    b, t, _ = x.shape
    row_spec = lambda n: pl.BlockSpec((1, tm, n), lambda i, j: (i, j, 0))
    if attn_feature_major:
        attn_spec = pl.BlockSpec((1, ATTN_WIDTH, tm), lambda i, j: (i, 0, j))
    else:
        attn_spec = row_spec(ATTN_WIDTH)
    return pl.pallas_call(
        functools.partial(_merge_kernel, attn_feature_major=attn_feature_major),
        grid=(b, t // tm),
        in_specs=[row_spec(D_MODEL), attn_spec, attn_spec, row_spec(D_MODEL), row_spec(D_MODEL),
                  _const_spec(wpa.shape), _const_spec(wout.shape)],
        out_specs=row_spec(D_MODEL),
        out_shape=jax.ShapeDtypeStruct((b, t, D_MODEL), F32),
        compiler_params=_compiler_params(("arbitrary", "arbitrary")),
        name="merge_fm" if attn_feature_major else "merge_rm",
    )(x, o, sza, sga, r, wpa, wout)


def _inproj_sample_kernel(x_ref, ng_ref, wq_ref, wk_ref, wv_ref, wza_ref, wf_ref,
                          wxl_ref, wzl_ref, wga_ref, wgb_ref,
                          qgrow_ref, kgrow_ref, bfrow_ref, bd_ref,
                          q_ref, k_ref, v_ref, sza_ref, lf_ref, xl_ref, szl_ref, sga_ref, sgb_ref):
    h = _rms_rows(x_ref[...], ng_ref[...]).astype(BF16)
    proj = lambda w_ref: lax.dot_general(h, w_ref[...], NT_DIMS, preferred_element_type=F32)
    q_ref[...] = _head_rms_rowmajor(proj(wq_ref), qgrow_ref[...], bd_ref[...]) * ATTN_SCALE
    k_ref[...] = _head_rms_rowmajor(proj(wk_ref), kgrow_ref[...], bd_ref[...])
    v_ref[...] = proj(wv_ref)
    sza_ref[...] = _silu(proj(wza_ref)).astype(BF16)
    lf_ref[...] = _log_sigmoid(proj(wf_ref) + bfrow_ref[...])
    xl_ref[...] = proj(wxl_ref)
    szl_ref[...] = _silu(proj(wzl_ref)).astype(BF16)
    sga_ref[...] = _sigmoid(proj(wga_ref)).astype(BF16)
    sgb_ref[...] = _sigmoid(proj(wgb_ref)).astype(BF16)


def _inproj_sample(x, ng, w, wf_pad, qg_row, kg_row, bf_row, bd, tm):
    n, _ = x.shape
    row_spec = lambda c: pl.BlockSpec((tm, c), lambda i: (i, 0))
    consts = (ng, w["q"], w["k"], w["v"], w["za"], wf_pad, w["xl"], w["zl"], w["ga"], w["gb"],
              qg_row, kg_row, bf_row, bd)
    widths = (ATTN_WIDTH, ATTN_WIDTH, ATTN_WIDTH, ATTN_WIDTH, V7X_LANES,
              LRU_WIDTH, LRU_WIDTH, D_MODEL, D_MODEL)
    dtypes = (F32, F32, F32, BF16, F32, F32, BF16, BF16, BF16)
    return pl.pallas_call(
        _inproj_sample_kernel,
        grid=(n // tm,),
        in_specs=[row_spec(D_MODEL)] + [_const_spec(c.shape) for c in consts],
        out_specs=tuple(row_spec(c) for c in widths),
        out_shape=tuple(jax.ShapeDtypeStruct((n, c), d) for c, d in zip(widths, dtypes)),
        compiler_params=_compiler_params(("arbitrary",)),
        name="inproj_sample",
    )(x, *consts)


def _fox_sample_kernel(pt_ref, q_ref, k_ref, v_ref, lft_ref, ck_hbm, cv_hbm, cl_hbm,
                       o_ref,
                       kbuf, vbuf, lbuf, cpbuf, ksem, vsem, lsem, *, pages_per_step):
    n_pages, n_seq = pt_ref.shape
    n_new = q_ref.shape[0] // n_seq
    pps = pages_per_step
    n_groups = n_pages // pps
    rows = n_new * N_HEADS
    keys = pps * PAGE_SIZE

    def kv_copies(b, g, slot):
        out = []
        for i in range(pps):
            page = pt_ref[g * pps + i, b]
            out.append(pltpu.make_async_copy(ck_hbm.at[page], kbuf.at[slot, i], ksem.at[slot]))
            out.append(pltpu.make_async_copy(cv_hbm.at[page], vbuf.at[slot, i], vsem.at[slot]))
        return out

    def lf_copies(b, slot):
        return [pltpu.make_async_copy(cl_hbm.at[pt_ref[p, b]], lbuf.at[slot, p], lsem.at[slot])
                for p in range(n_pages)]

    row_head = lax.broadcasted_iota(jnp.int32, (rows, ATTN_WIDTH), 0) % N_HEADS
    lane_head = lax.broadcasted_iota(jnp.int32, (rows, ATTN_WIDTH), 1) // HEAD_DIM
    head_mask = row_head == lane_head
    lane = lax.broadcasted_iota(jnp.int32, (N_HEADS, V7X_LANES), 1)
    tri = (lax.broadcasted_iota(jnp.int32, (PAGE_SIZE, PAGE_SIZE), 0)
           <= lax.broadcasted_iota(jnp.int32, (PAGE_SIZE, PAGE_SIZE), 1)).astype(BF16)

    for c in lf_copies(0, 0):
        c.start()
    for c in kv_copies(0, 0, 0):
        c.start()

    def per_sequence(b, _):
        lslot = b % 2
        for c in lf_copies(b, lslot):
            c.wait()

        @pl.when(b + 1 < n_seq)
        def _():
            for c in lf_copies(b + 1, 1 - lslot):
                c.start()

        def page_cumsum(p, carry):
            cp = _exact_dot(lbuf[lslot, p], tri) + carry
            cpbuf[p] = cp
            return jnp.broadcast_to(cp[:, PAGE_SIZE - 1:], cp.shape)
        total = lax.fori_loop(0, n_pages, page_cumsum, jnp.zeros((N_HEADS, PAGE_SIZE), F32))

        tok = pl.ds(pl.multiple_of(b * n_new, n_new), n_new)
        cn = lft_ref[b]
        for s in (1, 2, 4):
            cn = cn + jnp.where(lane >= s, pltpu.roll(cn, s, 1), 0.0)
        q_b = q_ref[tok, :]
        q_bd = jnp.concatenate(
            [jnp.broadcast_to(q_b[i:i + 1], (N_HEADS, ATTN_WIDTH)) for i in range(n_new)], axis=0)
        q_bd = jnp.where(head_mask, q_bd, 0.0).astype(BF16)
        cn_q = jnp.concatenate(
            [jnp.broadcast_to(cn[:, i:i + 1], (N_HEADS, V7X_LANES)) for i in range(n_new)], axis=0)

        def attend(carry, s, v_nt=None, v_nn=None):
            m, l, acc = carry
            m_new = jnp.maximum(m, jnp.max(s, axis=1, keepdims=True))
            alpha = jnp.exp(m - m_new)
            p = jnp.exp(s - m_new)
            l = alpha * l + jnp.sum(p, axis=1, keepdims=True)
            if v_nt is not None:
                pv = lax.dot_general(p.astype(BF16), v_nt, NT_DIMS, preferred_element_type=F32)
            else:
                pv = jnp.dot(p.astype(BF16), v_nn, preferred_element_type=F32)
            return m_new, l, alpha * acc + pv

        pad = jnp.zeros((V7X_LANES - n_new, ATTN_WIDTH), F32)
        k_new = jnp.concatenate([k_ref[tok, :], pad], axis=0).astype(BF16)
        v_new = jnp.concatenate([v_ref[tok, :], pad], axis=0).astype(BF16)
        s_new = lax.dot_general(q_bd, k_new, NT_DIMS, preferred_element_type=F32)
        cn_k = jnp.concatenate([cn] * n_new, axis=0)
        key_idx = lax.broadcasted_iota(jnp.int32, (rows, V7X_LANES), 1)
        qry_idx = lax.broadcasted_iota(jnp.int32, (rows, V7X_LANES), 0) // N_HEADS
        s_new = jnp.where(key_idx <= qry_idx, s_new + cn_q - cn_k, -jnp.inf)
        init = (jnp.full((rows, 1), -jnp.inf, F32), jnp.zeros((rows, 1), F32),
                jnp.zeros((rows, ATTN_WIDTH), F32))
        carry = attend(init, s_new, v_nn=v_new)

        def per_group(g, carry):
            slot = (b * n_groups + g) % 2
            for c in kv_copies(b, g, slot):
                c.wait()
            last = g == n_groups - 1
            nb = jnp.where(last, b + 1, b)
            ng = jnp.where(last, 0, g + 1)

            @pl.when(nb < n_seq)
            def _():
                for c in kv_copies(nb, ng, 1 - slot):
                    c.start()

            k_t = jnp.concatenate([kbuf[slot, i] for i in range(pps)], axis=1).astype(BF16)
            v_t = jnp.concatenate([vbuf[slot, i] for i in range(pps)], axis=1).astype(BF16)
            s = jnp.dot(q_bd, k_t, preferred_element_type=F32)
            bias = jnp.concatenate([total - cpbuf[g * pps + i] for i in range(pps)], axis=1)
            s = s + jnp.concatenate([cn_q] * pps, axis=1) + jnp.concatenate([bias] * n_new, axis=0)
            return attend(carry, s, v_nt=v_t)

        m, l, acc = lax.fori_loop(0, n_groups, per_group, carry)
        out = jnp.where(head_mask, acc / l, 0.0).reshape(n_new, N_HEADS, ATTN_WIDTH)
        o_ref[tok, :] = jnp.sum(out, axis=1)
        return 0

    lax.fori_loop(0, n_seq, per_sequence, 0)


def _fox_sample(pt_t, q, k, v, lf_t, cache_kt, cache_vt, cache_lt, pages_per_step):
    n_pages, n_seq = pt_t.shape
    vmem = pl.BlockSpec(memory_space=pltpu.VMEM)
    hbm = pl.BlockSpec(memory_space=pl.ANY)
    return pl.pallas_call(
        functools.partial(_fox_sample_kernel, pages_per_step=pages_per_step),
        in_specs=[pl.BlockSpec(memory_space=pltpu.SMEM), vmem, vmem, vmem, vmem, hbm, hbm, hbm],
        out_specs=vmem,
        out_shape=jax.ShapeDtypeStruct(q.shape, F32),
        scratch_shapes=[
            pltpu.VMEM((2, pages_per_step, ATTN_WIDTH, PAGE_SIZE), F32),
            pltpu.VMEM((2, pages_per_step, ATTN_WIDTH, PAGE_SIZE), F32),
            pltpu.VMEM((2, n_pages, N_HEADS, PAGE_SIZE), F32),
            pltpu.VMEM((n_pages, N_HEADS, PAGE_SIZE), F32),
            pltpu.SemaphoreType.DMA((2,)), pltpu.SemaphoreType.DMA((2,)),
            pltpu.SemaphoreType.DMA((2,)),
        ],
        compiler_params=pltpu.CompilerParams(vmem_limit_bytes=V7X_VMEM_LIMIT_BYTES),
        name="fox_sample",
    )(pt_t, q, k, v, lf_t, cache_kt, cache_vt, cache_lt)


INPROJ_ROWS = 512
LRU_ROWS = 256
ATTN_Q_ROWS = 512
ATTN_K_ROWS = 512
MERGE_ROWS = 512
SAMPLE_ROWS = 256
SAMPLE_PAGES_PER_STEP = 4

IN_SPLIT_NAMES = ("q", "k", "v", "za", "f", "xl", "zl", "ga", "gb")
IN_SPLIT_SIZES = (ATTN_WIDTH, ATTN_WIDTH, ATTN_WIDTH, ATTN_WIDTH, N_HEADS,
                  LRU_WIDTH, LRU_WIDTH, D_MODEL, D_MODEL)


def kernel(x_prompt, x_sample, cache_k, cache_v, cache_logf, state_conv, state_h, page_table, norm_gain, w_in, q_norm_gain, k_norm_gain, b_forget, conv_w, conv_b, w_rec_gate, b_rec_gate, w_in_gate, b_in_gate, lru_lambda, w_proj_attn, w_proj_lru, w_out):
    assert w_in.shape[0] == 1, "single-layer step"
    b, t, d = x_prompt.shape
    db, s, _ = x_sample.shape
    n_pool = cache_k.shape[1]

    wt = jnp.transpose(w_in[0]).astype(BF16)
    w, off = {}, 0
    for name, size in zip(IN_SPLIT_NAMES, IN_SPLIT_SIZES):
        w[name] = wt[off:off + size]
        off += size
    wf_pad = jnp.pad(w["f"], ((0, V7X_LANES - N_HEADS), (0, 0)))
    ng = norm_gain[0][None, :]
    qg, kg, bfg = q_norm_gain[0], k_norm_gain[0], b_forget[0]
    qg_row, kg_row = jnp.tile(qg, N_HEADS)[None, :], jnp.tile(kg, N_HEADS)[None, :]
    bf_row = jnp.pad(bfg, (0, V7X_LANES - N_HEADS))[None, :]
    head_of = jnp.arange(ATTN_WIDTH) // HEAD_DIM
    bd = (head_of[:, None] == head_of[None, :]).astype(BF16)
    wri = jnp.concatenate([w_rec_gate[0], w_in_gate[0]], axis=-1).astype(BF16)
    lru_consts = (conv_w[0], conv_b[0][None, :], wri, b_rec_gate[0][None, :],
                  b_in_gate[0][None, :], lru_lambda[0][None, :], w_proj_lru[0].astype(BF16))
    wpa, wout = w_proj_attn[0].astype(BF16), w_out[0].astype(BF16)

    qt, kp, kt, vt, vtb, szat, lft, xl, szl, sga, sgb = _inproj_prompt(
        x_prompt, ng, w, qg[:, None], kg[:, None], kg_row, bfg[:, None], bd, INPROJ_ROWS)
    r_p, hlast_p = _lru_prompt(xl, szl, sgb, *lru_consts, LRU_ROWS)
    ot = _fox_prompt(qt, kp, vtb, ATTN_Q_ROWS, ATTN_K_ROWS)
    y_prompt = _merge(x_prompt, ot, szat, sga, r_p, wpa, wout, MERGE_ROWS, True)
    to_tokens = lambda a: jnp.transpose(a.reshape(b, N_HEADS, HEAD_DIM, t), (0, 3, 1, 2))[None]
    k_prompt, v_prompt = to_tokens(kt), to_tokens(vt)
    logf_prompt = jnp.transpose(lft, (0, 2, 1))[None]
    conv_prompt = xl[:, t - (CONV_WIDTH - 1):, :][None]
    h_prompt = hlast_p[:, 0, :][None]

    xs = x_sample.reshape(db * s, d)
    q_s, k_s, v_s, sza_s, lf_s, xl_s, szl_s, sga_s, sgb_s = _inproj_sample(
        xs, ng, w, wf_pad, qg_row, kg_row, bf_row, bd, SAMPLE_ROWS)
    time_major = lambda a: jnp.transpose(a.reshape(db, s, -1), (1, 0, 2))
    conv0 = jnp.transpose(state_conv[0], (1, 0, 2))
    r_tm, hlast_s = _lru_sample(time_major(xl_s), conv0, state_h[0], time_major(szl_s),
                                time_major(sgb_s), *lru_consts)
    r_s = jnp.transpose(r_tm, (1, 0, 2)).reshape(1, db * s, d)
    lf_bsh = lf_s[:, :N_HEADS].reshape(db, s, N_HEADS)
    lf_t = jnp.pad(jnp.transpose(lf_bsh, (0, 2, 1)), ((0, 0), (0, 0), (0, V7X_LANES - s)))
    cache_kt = jnp.transpose(cache_k[0], (0, 2, 3, 1)).reshape(n_pool, ATTN_WIDTH, PAGE_SIZE)
    cache_vt = jnp.transpose(cache_v[0], (0, 2, 3, 1)).reshape(n_pool, ATTN_WIDTH, PAGE_SIZE)
    cache_lt = jnp.transpose(cache_logf[0], (0, 2, 1))
    o_s = _fox_sample(jnp.transpose(page_table), q_s, k_s, v_s, lf_t, cache_kt, cache_vt,
                      cache_lt, SAMPLE_PAGES_PER_STEP)
    y_sample = _merge(xs[None], o_s[None], sza_s[None], sga_s[None], r_s, wpa, wout,
                      SAMPLE_ROWS, False).reshape(db, s, d)
    k_sample = k_s.reshape(1, db, s, N_HEADS, HEAD_DIM)
    v_sample = v_s.reshape(1, db, s, N_HEADS, HEAD_DIM)
    logf_sample = lf_bsh[None]
    conv_sample = xl_s.reshape(db, s, LRU_WIDTH)[:, s - (CONV_WIDTH - 1):, :][None]
    h_sample = hlast_s[None]

    return (y_prompt, y_sample, k_prompt, v_prompt, logf_prompt, conv_prompt, h_prompt,
            k_sample, v_sample, logf_sample, conv_sample, h_sample)
```

```python
import functools

import jax
import jax.numpy as jnp
from jax import lax
from jax.experimental import pallas as pl
from jax.experimental.pallas import tpu as pltpu

F32 = jnp.float32
BF16 = jnp.bfloat16

D_MODEL = 1024
N_HEADS = 8
HEAD_DIM = 64
ATTN_WIDTH = N_HEADS * HEAD_DIM
LRU_WIDTH = D_MODEL
LRU_BLOCKS = 8
LRU_BLOCK = LRU_WIDTH // LRU_BLOCKS
CONV_WIDTH = 4
LRU_C = 8.0
NORM_EPS = 1e-6
ATTN_SCALE = HEAD_DIM ** -0.5
PAGE_SIZE = 128

V7X_LANES = 128
V7X_SUBLANES = 8
V7X_VMEM_LIMIT_BYTES = 56 * 1024 * 1024

HEAD_SLOT = 2 * HEAD_DIM
AUG_ONES = 3
V_SLOT = HEAD_DIM + 16
LOG2_E = 1.4426950408889634
MAX_SAFE_SHIFT = 50.0

NT_DIMS = (((1,), (1,)), ((), ()))


def _compiler_params(semantics):
    return pltpu.CompilerParams(dimension_semantics=semantics,
                                vmem_limit_bytes=V7X_VMEM_LIMIT_BYTES)


def _const_spec(shape):
    zeros = (0,) * len(shape)
    return pl.BlockSpec(shape, lambda *_: zeros, pipeline_mode=pl.Buffered(1))


def _split3(x):
    hi = x.astype(BF16).astype(F32)
    r = x - hi
    mid = r.astype(BF16).astype(F32)
    lo = (r - mid).astype(BF16).astype(F32)
    return hi, mid, lo


def _exact_dot(x, w):
    m = x.shape[0]
    parts = jnp.concatenate(_split3(x), axis=0).astype(BF16)
    y = jnp.dot(parts, w, preferred_element_type=F32)
    return y[:m] + y[m:2 * m] + y[2 * m:]


def _log_sigmoid(x):
    return jnp.minimum(x, 0.0) - jnp.log1p(jnp.exp(-jnp.abs(x)))


def _sigmoid(x):
    return jax.nn.sigmoid(x)


def _silu(x):
    return x * jax.nn.sigmoid(x)


def _rms_rows(x, gain_row):
    ms = jnp.mean(x * x, axis=-1, keepdims=True)
    return x * lax.rsqrt(ms + NORM_EPS) * gain_row


def _head_rms_rowmajor(u, gain_row, blockdiag_ones):
    ssq = jnp.dot((u * u).astype(BF16), blockdiag_ones, preferred_element_type=F32)
    return u * lax.rsqrt(ssq * (1.0 / HEAD_DIM) + NORM_EPS) * gain_row


def _inproj_prompt_kernel(x_ref, ng_ref, wq_ref, wk_ref, wv_ref, wza_ref, wf_ref,
                          wxl_ref, wzl_ref, wga_ref, wgb_ref,
                          qg_ref, kg_ref, kgrow_ref, bf_ref, shift_ref, bd_ref, tri_ref,
                          qt_ref, kp_ref, kt_ref, vt_ref, vtb_ref, szat_ref, lft_ref,
                          xl_ref, szl_ref, sga_ref, sgb_ref,
                          carry_ref):
    tm = x_ref.shape[1]

    @pl.when(pl.program_id(1) == 0)
    def _():
        carry_ref[...] = jnp.zeros_like(carry_ref)

    h = _rms_rows(x_ref[0], ng_ref[...]).astype(BF16)

    q_t = lax.dot_general(wq_ref[...], h, NT_DIMS, preferred_element_type=F32)
    k_t = lax.dot_general(wk_ref[...], h, NT_DIMS, preferred_element_type=F32)
    v_t = lax.dot_general(wv_ref[...], h, NT_DIMS, preferred_element_type=F32)
    za_t = lax.dot_general(wza_ref[...], h, NT_DIMS, preferred_element_type=F32)
    f_t = lax.dot_general(wf_ref[...], h, NT_DIMS, preferred_element_type=F32)

    logf_t = _log_sigmoid(f_t + bf_ref[...])
    lft_ref[0] = logf_t
    c_t = _exact_dot(logf_t, tri_ref[...]) + carry_ref[:, 0:1]
    carry_ref[...] = jnp.broadcast_to(c_t[:, tm - 1:tm], carry_ref.shape)
    c2_t = c_t * LOG2_E
    c_hi, c_mid, c_lo = _split3(c2_t)
    q_hi, q_mid, q_lo = _split3(c2_t - shift_ref[...])

    vt_ref[0] = v_t
    szat_ref[0] = _silu(za_t).astype(BF16)
    ones16 = jnp.ones((V_SLOT - HEAD_DIM, tm), F32)

    row8 = lax.broadcasted_iota(jnp.int32, (V7X_SUBLANES, tm), 0)
    ones_rows = jnp.where(row8 < AUG_ONES, 1.0, 0.0).astype(F32)
    zeros32 = jnp.zeros((HEAD_DIM - 4 * V7X_SUBLANES, tm), F32)
    for hd in range(N_HEADS):
        rows = slice(hd * HEAD_DIM, (hd + 1) * HEAD_DIM)
        qh = q_t[rows]
        qn = qh * lax.rsqrt(jnp.mean(qh * qh, axis=0, keepdims=True) + NORM_EPS) * qg_ref[...]
        qn = qn * (ATTN_SCALE * LOG2_E)
        kh = k_t[rows]
        kn = kh * lax.rsqrt(jnp.mean(kh * kh, axis=0, keepdims=True) + NORM_EPS) * kg_ref[...]
        kt_ref[0, rows, :] = kn
        vtb_ref[0, hd] = jnp.concatenate([v_t[rows], ones16], axis=0).astype(BF16)
        sel = jnp.where(row8 == hd, -1.0, 0.0).astype(F32)
        cq = jnp.where(row8 == 0, q_hi[hd:hd + 1],
                       jnp.where(row8 == 1, q_mid[hd:hd + 1],
                                 jnp.where(row8 == 2, q_lo[hd:hd + 1], 0.0)))
        aug = jnp.concatenate([cq, sel, sel, sel, zeros32], axis=0)
        if hd % 2 == 0:
            slot = jnp.concatenate([qn, aug], axis=0)
        else:
            slot = jnp.concatenate([aug, qn], axis=0)
        qt_ref[0, hd] = slot.astype(BF16)

    aug_k_t = jnp.concatenate([ones_rows, c_hi, c_mid, c_lo, zeros32], axis=0)
    aug_k = jnp.transpose(jnp.concatenate([aug_k_t, aug_k_t], axis=0))

    k_row = lax.dot_general(h, wk_ref[...], NT_DIMS, preferred_element_type=F32)
    k_row = _head_rms_rowmajor(k_row, kgrow_ref[...], bd_ref[...])
    lane = lax.broadcasted_iota(jnp.int32, (tm, HEAD_SLOT), 1)
    for pair in range(N_HEADS // 2):
        kcol = k_row[:, pair * HEAD_SLOT:(pair + 1) * HEAD_SLOT]
        even = jnp.where(lane < HEAD_DIM, kcol, aug_k)
        odd = jnp.where(lane < HEAD_DIM, aug_k, kcol)
        kp_ref[0, :, (2 * pair) * HEAD_SLOT:(2 * pair + 1) * HEAD_SLOT] = even.astype(BF16)
        kp_ref[0, :, (2 * pair + 1) * HEAD_SLOT:(2 * pair + 2) * HEAD_SLOT] = odd.astype(BF16)

    xl_ref[0] = lax.dot_general(h, wxl_ref[...], NT_DIMS, preferred_element_type=F32)
    szl_ref[0] = _silu(lax.dot_general(h, wzl_ref[...], NT_DIMS,
                                       preferred_element_type=F32)).astype(BF16)
    sga_ref[0] = _sigmoid(lax.dot_general(h, wga_ref[...], NT_DIMS,
                                          preferred_element_type=F32)).astype(BF16)
    sgb_ref[0] = _sigmoid(lax.dot_general(h, wgb_ref[...], NT_DIMS,
                                          preferred_element_type=F32)).astype(BF16)


def _inproj_prompt(x, ng, w, qg_col, kg_col, kg_row, bf_col, shift_col, bd, tm):
    b, t, _ = x.shape
    tri = jnp.triu(jnp.ones((tm, tm), F32)).astype(BF16)
    row_spec = lambda n: pl.BlockSpec((1, tm, n), lambda i, j: (i, j, 0))
    col_spec = lambda n: pl.BlockSpec((1, n, tm), lambda i, j: (i, 0, j))
    out_shape = (
        jax.ShapeDtypeStruct((b, N_HEADS, HEAD_SLOT, t), BF16),
        jax.ShapeDtypeStruct((b, t, N_HEADS * HEAD_SLOT), BF16),
        jax.ShapeDtypeStruct((b, ATTN_WIDTH, t), F32),
        jax.ShapeDtypeStruct((b, ATTN_WIDTH, t), F32),
        jax.ShapeDtypeStruct((b, N_HEADS, V_SLOT, t), BF16),
        jax.ShapeDtypeStruct((b, ATTN_WIDTH, t), BF16),
        jax.ShapeDtypeStruct((b, N_HEADS, t), F32),
        jax.ShapeDtypeStruct((b, t, LRU_WIDTH), F32),
        jax.ShapeDtypeStruct((b, t, LRU_WIDTH), BF16),
        jax.ShapeDtypeStruct((b, t, D_MODEL), BF16),
        jax.ShapeDtypeStruct((b, t, D_MODEL), BF16),
    )
    out_specs = (
        pl.BlockSpec((1, N_HEADS, HEAD_SLOT, tm), lambda i, j: (i, 0, 0, j)),
        row_spec(N_HEADS * HEAD_SLOT),
        col_spec(ATTN_WIDTH), col_spec(ATTN_WIDTH),
        pl.BlockSpec((1, N_HEADS, V_SLOT, tm), lambda i, j: (i, 0, 0, j)),
        col_spec(ATTN_WIDTH), col_spec(N_HEADS),
        row_spec(LRU_WIDTH), row_spec(LRU_WIDTH), row_spec(D_MODEL), row_spec(D_MODEL),
    )
    consts = (ng, w["q"], w["k"], w["v"], w["za"], w["f"], w["xl"], w["zl"], w["ga"], w["gb"],
              qg_col, kg_col, kg_row, bf_col, shift_col, bd, tri)
    return pl.pallas_call(
        _inproj_prompt_kernel,
        grid=(b, t // tm),
        in_specs=[row_spec(D_MODEL)] + [_const_spec(c.shape) for c in consts],
        out_specs=out_specs,
        out_shape=out_shape,
        scratch_shapes=[pltpu.VMEM((N_HEADS, V7X_LANES), F32)],
        compiler_params=_compiler_params(("arbitrary", "arbitrary")),
        name="inproj_prompt",
    )(x, *consts)


def _lru_gates(xc, wri_ref, br_row, bi_row, log_sig_lam_row):
    xb = xc.astype(BF16)
    pre_r, pre_i = [], []
    for g in range(LRU_BLOCKS):
        cols = slice(g * LRU_BLOCK, (g + 1) * LRU_BLOCK)
        ri = jnp.dot(xb[:, cols], wri_ref[g], preferred_element_type=F32)
        pre_r.append(ri[:, :LRU_BLOCK])
        pre_i.append(ri[:, LRU_BLOCK:])
    r = _sigmoid(jnp.concatenate(pre_r, axis=1) + br_row)
    i = _sigmoid(jnp.concatenate(pre_i, axis=1) + bi_row)
    log_a = LRU_C * r * log_sig_lam_row
    a = jnp.exp(log_a)
    b = jnp.sqrt(-jnp.tanh(log_a) * (1.0 + a * a)) * (i * xc)
    return a, b


def _lru_prompt_kernel(xl_ref, szl_ref, sgb_ref, cw_ref, cb_ref, wri_ref, br_ref, bi_ref,
                       lam_ref, wpl_ref,
                       r_ref, hlast_ref,
                       xprev_ref, hprev_ref, a_ref, b_ref, h_ref):
    tl = xl_ref.shape[1]
    nblk = tl // V7X_SUBLANES

    @pl.when(pl.program_id(1) == 0)
    def _():
        xprev_ref[...] = jnp.zeros_like(xprev_ref)
        hprev_ref[...] = jnp.zeros_like(hprev_ref)

    x = xl_ref[0]
    prev = xprev_ref[...]
    row8 = lax.broadcasted_iota(jnp.int32, (V7X_SUBLANES, LRU_WIDTH), 0)
    xc = cb_ref[...] + x * cw_ref[CONV_WIDTH - 1:CONV_WIDTH, :]
    for back in range(1, CONV_WIDTH):
        w_row = cw_ref[CONV_WIDTH - 1 - back:CONV_WIDTH - back, :]
        rolled = pltpu.roll(x, back, 0)
        head = jnp.where(row8 < back, pltpu.roll(prev, back, 0), rolled[:V7X_SUBLANES])
        shifted = jnp.concatenate([head, rolled[V7X_SUBLANES:]], axis=0)
        xc = xc + shifted * w_row
    xprev_ref[...] = x[tl - V7X_SUBLANES:]

    a, b = _lru_gates(xc, wri_ref, br_ref[...], bi_ref[...], _log_sigmoid(lam_ref[...]))
    a_ref[...] = a
    b_ref[...] = b

    def block(k, hprev):
        rows = pl.ds(pl.multiple_of(k * V7X_SUBLANES, V7X_SUBLANES), V7X_SUBLANES)
        ak = a_ref[rows, :]
        bk = b_ref[rows, :]
        for s in (1, 2, 4):
            keep = row8 >= s
            b_sh = pltpu.roll(bk, s, 0)
            a_sh = pltpu.roll(ak, s, 0)
            bk = jnp.where(keep, ak * b_sh + bk, bk)
            ak = jnp.where(keep, ak * a_sh, ak)
        hk = ak * hprev + bk
        h_ref[rows, :] = hk
        return jnp.broadcast_to(hk[V7X_SUBLANES - 1:], hk.shape)

    hlast = lax.fori_loop(0, nblk, block, hprev_ref[...])
    hprev_ref[...] = hlast
    hlast_ref[0] = hlast

    g = (h_ref[...] * szl_ref[0].astype(F32)).astype(BF16)
    r_ref[0] = sgb_ref[0].astype(F32) * jnp.dot(g, wpl_ref[...], preferred_element_type=F32)


def _lru_prompt(xl, szl, sgb, cw, cb, wri, br, bi, lam, wpl, tl):
    b, t, c = xl.shape
    row_spec = pl.BlockSpec((1, tl, c), lambda i, j: (i, j, 0))
    consts = (cw, cb, wri, br, bi, lam, wpl)
    return pl.pallas_call(
        _lru_prompt_kernel,
        grid=(b, t // tl),
        in_specs=[row_spec, row_spec, row_spec] + [_const_spec(x.shape) for x in consts],
        out_specs=(row_spec, pl.BlockSpec((1, V7X_SUBLANES, c), lambda i, j: (i, 0, 0))),
        out_shape=(jax.ShapeDtypeStruct((b, t, D_MODEL), F32),
                   jax.ShapeDtypeStruct((b, V7X_SUBLANES, c), F32)),
        scratch_shapes=[pltpu.VMEM((V7X_SUBLANES, c), F32), pltpu.VMEM((V7X_SUBLANES, c), F32),
                        pltpu.VMEM((tl, c), F32), pltpu.VMEM((tl, c), F32), pltpu.VMEM((tl, c), F32)],
        compiler_params=_compiler_params(("arbitrary", "arbitrary")),
        name="lru_prompt",
    )(xl, szl, sgb, *consts)


def _lru_sample_kernel(xl_ref, conv0_ref, h0_ref, szl_ref, sgb_ref, cw_ref, cb_ref, wri_ref,
                       br_ref, bi_ref, lam_ref, wpl_ref,
                       r_ref, hlast_ref):
    s, n, c = xl_ref.shape
    slabs = [conv0_ref[j] for j in range(CONV_WIDTH - 1)] + [xl_ref[t] for t in range(s)]
    xc = []
    for t in range(s):
        acc = cb_ref[...] + slabs[t] * cw_ref[0:1, :]
        for j in range(1, CONV_WIDTH):
            acc = acc + slabs[t + j] * cw_ref[j:j + 1, :]
        xc.append(acc)
    xc = jnp.concatenate(xc, axis=0)
    a, b = _lru_gates(xc, wri_ref, br_ref[...], bi_ref[...], _log_sigmoid(lam_ref[...]))
    h = h0_ref[...]
    hs = []
    for t in range(s):
        h = a[t * n:(t + 1) * n] * h + b[t * n:(t + 1) * n]
        hs.append(h)
    hlast_ref[...] = h
    g = (jnp.concatenate(hs, axis=0) * szl_ref[...].reshape(s * n, c).astype(F32)).astype(BF16)
    r = jnp.dot(g, wpl_ref[...], preferred_element_type=F32)
    r_ref[...] = (sgb_ref[...].reshape(s * n, c).astype(F32) * r).reshape(s, n, c)


def _lru_sample(xl, conv0, h0, szl, sgb, cw, cb, wri, br, bi, lam, wpl):
    s, n, c = xl.shape
    return pl.pallas_call(
        _lru_sample_kernel,
        out_shape=(jax.ShapeDtypeStruct((s, n, D_MODEL), F32),
                   jax.ShapeDtypeStruct((n, c), F32)),
        compiler_params=pltpu.CompilerParams(vmem_limit_bytes=V7X_VMEM_LIMIT_BYTES),
        name="lru_sample",
    )(xl, conv0, h0, szl, sgb, cw, cb, wri, br, bi, lam, wpl)


def _fox_prompt_shifted_kernel(qt_ref, kp_ref, vt_ref, ot_ref, p_ref, *, n_sub):
    tq = qt_ref.shape[3]
    ts = tq // n_sub
    qi = pl.program_id(2)
    q_t = qt_ref[0, 0]
    key_pos = lax.broadcasted_iota(jnp.int32, (ts, tq), 0)
    qry_pos = lax.broadcasted_iota(jnp.int32, (ts, tq), 1)

    def probs(k0, slot, masked):
        for u in range(n_sub):
            ku = pl.multiple_of(k0 + u * ts, ts)
            s = jnp.dot(kp_ref[0, pl.ds(ku, ts), :], q_t, preferred_element_type=F32)
            if masked:
                s = jnp.where(key_pos + u * ts <= qry_pos, s, -jnp.inf)
            p_ref[slot, u * ts:(u + 1) * ts, :] = jnp.exp2(s).astype(BF16)

    def weighted(k0, slot):
        out = None
        for u in range(n_sub):
            ku = pl.multiple_of(k0 + u * ts, ts)
            pv = jnp.dot(vt_ref[0, 0, :, pl.ds(ku, ts)], p_ref[slot, u * ts:(u + 1) * ts, :],
                         preferred_element_type=F32)
            out = pv if out is None else out + pv
        return out

    diag0 = pl.multiple_of(qi * tq, tq)
    probs(diag0, 0, True)

    def body(j, carry):
        acc, prev = carry
        pv = weighted(prev, j % 2)
        k0 = pl.multiple_of(j * tq, tq)
        probs(k0, (j + 1) % 2, False)
        return acc + pv, k0

    acc, prev = lax.fori_loop(0, qi, body, (jnp.zeros((V_SLOT, tq), F32), diag0))
    acc = acc + weighted(prev, qi % 2)
    ot_ref[0] = acc[:HEAD_DIM] / acc[HEAD_DIM:HEAD_DIM + 1]


def _fox_prompt_online_kernel(qt_ref, kp_ref, vt_ref, ot_ref):
    tq = qt_ref.shape[3]
    qi = pl.program_id(2)
    q_t = qt_ref[0, 0]

    def step(k0, carry, mask):
        m, l, acc = carry
        s = jnp.dot(kp_ref[0, pl.ds(k0, tq), :], q_t, preferred_element_type=F32)
        if mask is not None:
            s = jnp.where(mask, s, -jnp.inf)
        m_new = jnp.maximum(m, jnp.max(s, axis=0, keepdims=True))
        alpha = jnp.exp2(m - m_new)
        p = jnp.exp2(s - m_new)
        l = alpha * l + jnp.sum(p, axis=0, keepdims=True)
        v_blk = vt_ref[0, 0, 0:HEAD_DIM, pl.ds(k0, tq)]
        acc = alpha * acc + jnp.dot(v_blk, p.astype(BF16), preferred_element_type=F32)
        return m_new, l, acc

    init = (jnp.full((1, tq), -jnp.inf, F32), jnp.zeros((1, tq), F32),
            jnp.zeros((HEAD_DIM, tq), F32))
    carry = lax.fori_loop(0, qi, lambda j, c: step(pl.multiple_of(j * tq, tq), c, None), init)
    key_pos = lax.broadcasted_iota(jnp.int32, (tq, tq), 0)
    qry_pos = lax.broadcasted_iota(jnp.int32, (tq, tq), 1)
    m, l, acc = step(pl.multiple_of(qi * tq, tq), carry, key_pos <= qry_pos)
    ot_ref[0] = acc / l


def _fox_prompt(qt, kp, vtb, tq, n_sub, shifted):
    b, nh, _, t = qt.shape
    if shifted:
        body = functools.partial(_fox_prompt_shifted_kernel, n_sub=n_sub)
        scratch = [pltpu.VMEM((2, tq, tq), BF16)]
    else:
        body, scratch = _fox_prompt_online_kernel, []
    return pl.pallas_call(
        body,
        grid=(b, nh, t // tq),
        in_specs=[pl.BlockSpec((1, 1, HEAD_SLOT, tq), lambda i, h, j: (i, h, 0, j)),
                  pl.BlockSpec((1, t, HEAD_SLOT), lambda i, h, j: (i, 0, h)),
                  pl.BlockSpec((1, 1, V_SLOT, t), lambda i, h, j: (i, h, 0, 0))],
        out_specs=pl.BlockSpec((1, HEAD_DIM, tq), lambda i, h, j: (i, h, j)),
        out_shape=jax.ShapeDtypeStruct((b, ATTN_WIDTH, t), F32),
        scratch_shapes=scratch,
        compiler_params=_compiler_params(("arbitrary", "arbitrary", "arbitrary")),
        name="fox_prompt_shifted" if shifted else "fox_prompt_online",
    )(qt, kp, vtb)


def _merge_kernel(x_ref, o_ref, sza_ref, sga_ref, r_ref, wpa_ref, wout_ref, y_ref, *,
                  attn_feature_major):
    g = o_ref[0] * sza_ref[0].astype(F32)
    if attn_feature_major:
        g = jnp.transpose(g)
    a = jnp.dot(g.astype(BF16), wpa_ref[...], preferred_element_type=F32)
    mixed = sga_ref[0].astype(F32) * a + r_ref[0]
    y_ref[0] = x_ref[0] + jnp.dot(mixed.astype(BF16), wout_ref[...], preferred_element_type=F32)


def _merge(x, o, sza, sga, r, wpa, wout, tm, attn_feature_major):
    b, t, _ = x.shape
    row_spec = lambda n: pl.BlockSpec((1, tm, n), lambda i, j: (i, j, 0))
    if attn_feature_major:
        attn_spec = pl.BlockSpec((1, ATTN_WIDTH, tm), lambda i, j: (i, 0, j))
    else:
        attn_spec = row_spec(ATTN_WIDTH)
    return pl.pallas_call(
        functools.partial(_merge_kernel, attn_feature_major=attn_feature_major),
        grid=(b, t // tm),
        in_specs=[row_spec(D_MODEL), attn_spec, attn_spec, row_spec(D_MODEL), row_spec(D_MODEL),
                  _const_spec(wpa.shape), _const_spec(wout.shape)],
        out_specs=row_spec(D_MODEL),
        out_shape=jax.ShapeDtypeStruct((b, t, D_MODEL), F32),
        compiler_params=_compiler_params(("arbitrary", "arbitrary")),
        name="merge_fm" if attn_feature_major else "merge_rm",
    )(x, o, sza, sga, r, wpa, wout)


def _inproj_sample_kernel(x_ref, ng_ref, wq_ref, wk_ref, wv_ref, wza_ref, wf_ref,
                          wxl_ref, wzl_ref, wga_ref, wgb_ref,
                          qgrow_ref, kgrow_ref, bfrow_ref, bd_ref,
                          q_ref, k_ref, v_ref, sza_ref, lf_ref, xl_ref, szl_ref, sga_ref, sgb_ref):
    h = _rms_rows(x_ref[...], ng_ref[...]).astype(BF16)
    proj = lambda w_ref: lax.dot_general(h, w_ref[...], NT_DIMS, preferred_element_type=F32)
    q_ref[...] = _head_rms_rowmajor(proj(wq_ref), qgrow_ref[...], bd_ref[...]) * ATTN_SCALE
    k_ref[...] = _head_rms_rowmajor(proj(wk_ref), kgrow_ref[...], bd_ref[...])
    v_ref[...] = proj(wv_ref)
    sza_ref[...] = _silu(proj(wza_ref)).astype(BF16)
    lf_ref[...] = _log_sigmoid(proj(wf_ref) + bfrow_ref[...])
    xl_ref[...] = proj(wxl_ref)
    szl_ref[...] = _silu(proj(wzl_ref)).astype(BF16)
    sga_ref[...] = _sigmoid(proj(wga_ref)).astype(BF16)
    sgb_ref[...] = _sigmoid(proj(wgb_ref)).astype(BF16)


def _inproj_sample(x, ng, w, wf_pad, qg_row, kg_row, bf_row, bd, tm):
    n, _ = x.shape
    row_spec = lambda c: pl.BlockSpec((tm, c), lambda i: (i, 0))
    consts = (ng, w["q"], w["k"], w["v"], w["za"], wf_pad, w["xl"], w["zl"], w["ga"], w["gb"],
              qg_row, kg_row, bf_row, bd)
    widths = (ATTN_WIDTH, ATTN_WIDTH, ATTN_WIDTH, ATTN_WIDTH, V7X_LANES,
              LRU_WIDTH, LRU_WIDTH, D_MODEL, D_MODEL)
    dtypes = (F32, F32, F32, BF16, F32, F32, BF16, BF16, BF16)
    return pl.pallas_call(
        _inproj_sample_kernel,
        grid=(n // tm,),
        in_specs=[row_spec(D_MODEL)] + [_const_spec(c.shape) for c in consts],
        out_specs=tuple(row_spec(c) for c in widths),
        out_shape=tuple(jax.ShapeDtypeStruct((n, c), d) for c, d in zip(widths, dtypes)),
        compiler_params=_compiler_params(("arbitrary",)),
        name="inproj_sample",
    )(x, *consts)


def _fox_sample_kernel(pt_ref, q_ref, k_ref, v_ref, lft_ref, ck_hbm, cv_hbm, cl_hbm,
                       o_ref,
                       kbuf, vbuf, lbuf, cpbuf, pre_ref, ksem, vsem, lsem, *,
                       pages_per_step, ring_slots):
    n_pages, n_seq = pt_ref.shape
    n_new = q_ref.shape[0] // n_seq
    pps = pages_per_step
    n_groups = n_pages // pps
    assert n_groups & (n_groups - 1) == 0, "page groups per sequence must be a power of two"
    group_shift = n_groups.bit_length() - 1
    n_steps = n_seq * n_groups
    lookahead = ring_slots - 1
    rows = n_new * N_HEADS
    page_rows = n_pages * N_HEADS

    def kv_copies(b, g, slot):
        out = []
        for i in range(pps):
            page = pt_ref[g * pps + i, b]
            out.append(pltpu.make_async_copy(ck_hbm.at[page], kbuf.at[slot, i], ksem.at[slot]))
            out.append(pltpu.make_async_copy(cv_hbm.at[page], vbuf.at[slot, i], vsem.at[slot]))
        return out

    def lf_copies(b, slot):
        return [pltpu.make_async_copy(cl_hbm.at[pt_ref[p, b]], lbuf.at[slot, p], lsem.at[slot])
                for p in range(n_pages)]

    row_head = lax.broadcasted_iota(jnp.int32, (rows, ATTN_WIDTH), 0) % N_HEADS
    lane_head = lax.broadcasted_iota(jnp.int32, (rows, ATTN_WIDTH), 1) // HEAD_DIM
    head_mask = row_head == lane_head
    lane = lax.broadcasted_iota(jnp.int32, (N_HEADS, V7X_LANES), 1)
    tri = (lax.broadcasted_iota(jnp.int32, (PAGE_SIZE, PAGE_SIZE), 0)
           <= lax.broadcasted_iota(jnp.int32, (PAGE_SIZE, PAGE_SIZE), 1)).astype(BF16)

    pr = lax.broadcasted_iota(jnp.int32, (page_rows, page_rows), 0)
    pc = lax.broadcasted_iota(jnp.int32, (page_rows, page_rows), 1)
    pre_ref[...] = jnp.where(pr % N_HEADS == pc % N_HEADS,
                             jnp.where(pc // N_HEADS < pr // N_HEADS, 1.0, 0.0), 0.0).astype(BF16)

    for c in lf_copies(0, 0):
        c.start()
    assert lookahead <= n_groups
    for g0 in range(lookahead):
        for c in kv_copies(0, g0, g0):
            c.start()

    def per_sequence(b, _):
        lslot = b % 2
        for c in lf_copies(b, lslot):
            c.wait()

        @pl.when(b + 1 < n_seq)
        def _():
            for c in lf_copies(b + 1, 1 - lslot):
                c.start()

        lf = lbuf[lslot].reshape(page_rows, PAGE_SIZE)
        within = _exact_dot(lf, tri)
        page_tot = jnp.broadcast_to(within[:, PAGE_SIZE - 1:], within.shape)
        parts = jnp.concatenate(_split3(page_tot), axis=1).astype(BF16)
        before = jnp.dot(pre_ref[...], parts, preferred_element_type=F32)
        cp = (within + before[:, :PAGE_SIZE] + before[:, PAGE_SIZE:2 * PAGE_SIZE]
              + before[:, 2 * PAGE_SIZE:])
        total = jnp.broadcast_to(cp[page_rows - N_HEADS:, PAGE_SIZE - 1:], (N_HEADS, PAGE_SIZE))
        after = jnp.concatenate([total] * n_pages, axis=0) - cp
        cpbuf[...] = after.reshape(n_pages, N_HEADS, PAGE_SIZE)

        tok = pl.ds(pl.multiple_of(b * n_new, n_new), n_new)
        cn = lft_ref[b]
        for s in (1, 2, 4):
            cn = cn + jnp.where(lane >= s, pltpu.roll(cn, s, 1), 0.0)
        q_b = q_ref[tok, :]
        q_bd = jnp.concatenate(
            [jnp.broadcast_to(q_b[i:i + 1], (N_HEADS, ATTN_WIDTH)) for i in range(n_new)], axis=0)
        q_bd = jnp.where(head_mask, q_bd, 0.0).astype(BF16)
        cn_q = jnp.concatenate(
            [jnp.broadcast_to(cn[:, i:i + 1], (N_HEADS, V7X_LANES)) for i in range(n_new)], axis=0)

        def attend(carry, s, v_nt=None, v_nn=None):
            m, l, acc = carry
            m_new = jnp.maximum(m, jnp.max(s, axis=1, keepdims=True))
            alpha = jnp.exp(m - m_new)
            p = jnp.exp(s - m_new)
            l = alpha * l + jnp.sum(p, axis=1, keepdims=True)
            if v_nt is not None:
                pv = lax.dot_general(p.astype(BF16), v_nt, NT_DIMS, preferred_element_type=F32)
            else:
                pv = jnp.dot(p.astype(BF16), v_nn, preferred_element_type=F32)
            return m_new, l, alpha * acc + pv

        pad = jnp.zeros((V7X_LANES - n_new, ATTN_WIDTH), F32)
        k_new = jnp.concatenate([k_ref[tok, :], pad], axis=0).astype(BF16)
        v_new = jnp.concatenate([v_ref[tok, :], pad], axis=0).astype(BF16)
        s_new = lax.dot_general(q_bd, k_new, NT_DIMS, preferred_element_type=F32)
        cn_k = jnp.concatenate([cn] * n_new, axis=0)
        key_idx = lax.broadcasted_iota(jnp.int32, (rows, V7X_LANES), 1)
        qry_idx = lax.broadcasted_iota(jnp.int32, (rows, V7X_LANES), 0) // N_HEADS
        s_new = jnp.where(key_idx <= qry_idx, s_new + cn_q - cn_k, -jnp.inf)
        init = (jnp.full((rows, 1), -jnp.inf, F32), jnp.zeros((rows, 1), F32),
                jnp.zeros((rows, ATTN_WIDTH), F32))
        carry = attend(init, s_new, v_nn=v_new)

        def per_group(g, carry):
            step = b * n_groups + g
            slot = step % ring_slots
            for c in kv_copies(b, g, slot):
                c.wait()
            ahead = step + lookahead

            @pl.when(ahead < n_steps)
            def _():
                for c in kv_copies(lax.shift_right_logical(ahead, group_shift),
                                   ahead & (n_groups - 1), ahead % ring_slots):
                    c.start()

            k_t = jnp.concatenate([kbuf[slot, i] for i in range(pps)], axis=1).astype(BF16)
            v_t = jnp.concatenate([vbuf[slot, i] for i in range(pps)], axis=1).astype(BF16)
            s = jnp.dot(q_bd, k_t, preferred_element_type=F32)
            bias = jnp.concatenate([cpbuf[g * pps + i] for i in range(pps)], axis=1)
            s = s + jnp.concatenate([cn_q] * pps, axis=1) + jnp.concatenate([bias] * n_new, axis=0)
            return attend(carry, s, v_nt=v_t)

        m, l, acc = lax.fori_loop(0, n_groups, per_group, carry)
        out = jnp.where(head_mask, acc / l, 0.0).reshape(n_new, N_HEADS, ATTN_WIDTH)
        o_ref[tok, :] = jnp.sum(out, axis=1)
        return 0

    lax.fori_loop(0, n_seq, per_sequence, 0)


def _fox_sample(pt_t, q, k, v, lf_t, cache_kt, cache_vt, cache_lt, pages_per_step, ring_slots):
    n_pages, n_seq = pt_t.shape
    vmem = pl.BlockSpec(memory_space=pltpu.VMEM)
    hbm = pl.BlockSpec(memory_space=pl.ANY)
    return pl.pallas_call(
        functools.partial(_fox_sample_kernel, pages_per_step=pages_per_step, ring_slots=ring_slots),
        in_specs=[pl.BlockSpec(memory_space=pltpu.SMEM), vmem, vmem, vmem, vmem, hbm, hbm, hbm],
        out_specs=vmem,
        out_shape=jax.ShapeDtypeStruct(q.shape, F32),
        scratch_shapes=[
            pltpu.VMEM((ring_slots, pages_per_step, ATTN_WIDTH, PAGE_SIZE), F32),
            pltpu.VMEM((ring_slots, pages_per_step, ATTN_WIDTH, PAGE_SIZE), F32),
            pltpu.VMEM((2, n_pages, N_HEADS, PAGE_SIZE), F32),
            pltpu.VMEM((n_pages, N_HEADS, PAGE_SIZE), F32),
            pltpu.VMEM((n_pages * N_HEADS, n_pages * N_HEADS), BF16),
            pltpu.SemaphoreType.DMA((ring_slots,)), pltpu.SemaphoreType.DMA((ring_slots,)),
            pltpu.SemaphoreType.DMA((2,)),
        ],
        compiler_params=pltpu.CompilerParams(vmem_limit_bytes=V7X_VMEM_LIMIT_BYTES),
        name="fox_sample",
    )(pt_t, q, k, v, lf_t, cache_kt, cache_vt, cache_lt)


INPROJ_ROWS = 512
LRU_ROWS = 256
ATTN_Q_ROWS = 512
ATTN_SUB_BLOCKS = 2
MERGE_ROWS = 512
SAMPLE_ROWS = 256
SAMPLE_PAGES_PER_STEP = 8
SAMPLE_RING_SLOTS = 3

IN_SPLIT_NAMES = ("q", "k", "v", "za", "f", "xl", "zl", "ga", "gb")
IN_SPLIT_SIZES = (ATTN_WIDTH, ATTN_WIDTH, ATTN_WIDTH, ATTN_WIDTH, N_HEADS,
                  LRU_WIDTH, LRU_WIDTH, D_MODEL, D_MODEL)


def kernel(x_prompt, x_sample, cache_k, cache_v, cache_logf, state_conv, state_h, page_table, norm_gain, w_in, q_norm_gain, k_norm_gain, b_forget, conv_w, conv_b, w_rec_gate, b_rec_gate, w_in_gate, b_in_gate, lru_lambda, w_proj_attn, w_proj_lru, w_out):
    assert w_in.shape[0] == 1, "single-layer step"
    b, t, d = x_prompt.shape
    db, s, _ = x_sample.shape
    n_pool = cache_k.shape[1]

    wt = jnp.transpose(w_in[0]).astype(BF16)
    w, off = {}, 0
    for name, size in zip(IN_SPLIT_NAMES, IN_SPLIT_SIZES):
        w[name] = wt[off:off + size]
        off += size
    wf_pad = jnp.pad(w["f"], ((0, V7X_LANES - N_HEADS), (0, 0)))
    ng = norm_gain[0][None, :]
    qg, kg, bfg = q_norm_gain[0], k_norm_gain[0], b_forget[0]
    qg_row, kg_row = jnp.tile(qg, N_HEADS)[None, :], jnp.tile(kg, N_HEADS)[None, :]
    bf_row = jnp.pad(bfg, (0, V7X_LANES - N_HEADS))[None, :]
    head_of = jnp.arange(ATTN_WIDTH) // HEAD_DIM
    bd = (head_of[:, None] == head_of[None, :]).astype(BF16)
    wri = jnp.concatenate([w_rec_gate[0], w_in_gate[0]], axis=-1).astype(BF16)
    lru_consts = (conv_w[0], conv_b[0][None, :], wri, b_rec_gate[0][None, :],
                  b_in_gate[0][None, :], lru_lambda[0][None, :], w_proj_lru[0].astype(BF16))
    wpa, wout = w_proj_attn[0].astype(BF16), w_out[0].astype(BF16)

    score_bound = (LOG2_E * ATTN_SCALE * HEAD_DIM) * jnp.max(jnp.abs(qg)) * jnp.max(jnp.abs(kg))
    shift_is_safe = score_bound <= MAX_SAFE_SHIFT
    shift_col = jnp.broadcast_to(jnp.where(shift_is_safe, score_bound, 0.0), (N_HEADS, 1))
    qt, kp, kt, vt, vtb, szat, lft, xl, szl, sga, sgb = _inproj_prompt(
        x_prompt, ng, w, qg[:, None], kg[:, None], kg_row, bfg[:, None], shift_col, bd, INPROJ_ROWS)
    r_p, hlast_p = _lru_prompt(xl, szl, sgb, *lru_consts, LRU_ROWS)
    ot = lax.cond(shift_is_safe,
                  lambda: _fox_prompt(qt, kp, vtb, ATTN_Q_ROWS, ATTN_SUB_BLOCKS, True),
                  lambda: _fox_prompt(qt, kp, vtb, ATTN_Q_ROWS, ATTN_SUB_BLOCKS, False))
    y_prompt = _merge(x_prompt, ot, szat, sga, r_p, wpa, wout, MERGE_ROWS, True)
    to_tokens = lambda a: jnp.transpose(a.reshape(b, N_HEADS, HEAD_DIM, t), (0, 3, 1, 2))[None]
    k_prompt, v_prompt = to_tokens(kt), to_tokens(vt)
    logf_prompt = jnp.transpose(lft, (0, 2, 1))[None]
    conv_prompt = xl[:, t - (CONV_WIDTH - 1):, :][None]
    h_prompt = hlast_p[:, 0, :][None]

    xs = x_sample.reshape(db * s, d)
    q_s, k_s, v_s, sza_s, lf_s, xl_s, szl_s, sga_s, sgb_s = _inproj_sample(
        xs, ng, w, wf_pad, qg_row, kg_row, bf_row, bd, SAMPLE_ROWS)
    time_major = lambda a: jnp.transpose(a.reshape(db, s, -1), (1, 0, 2))
    conv0 = jnp.transpose(state_conv[0], (1, 0, 2))
    r_tm, hlast_s = _lru_sample(time_major(xl_s), conv0, state_h[0], time_major(szl_s),
                                time_major(sgb_s), *lru_consts)
    r_s = jnp.transpose(r_tm, (1, 0, 2)).reshape(1, db * s, d)
    lf_bsh = lf_s[:, :N_HEADS].reshape(db, s, N_HEADS)
    lf_t = jnp.pad(jnp.transpose(lf_bsh, (0, 2, 1)), ((0, 0), (0, 0), (0, V7X_LANES - s)))
    cache_kt = jnp.transpose(cache_k[0], (0, 2, 3, 1)).reshape(n_pool, ATTN_WIDTH, PAGE_SIZE)
    cache_vt = jnp.transpose(cache_v[0], (0, 2, 3, 1)).reshape(n_pool, ATTN_WIDTH, PAGE_SIZE)
    cache_lt = jnp.transpose(cache_logf[0], (0, 2, 1))
    o_s = _fox_sample(jnp.transpose(page_table), q_s, k_s, v_s, lf_t, cache_kt, cache_vt,
                      cache_lt, SAMPLE_PAGES_PER_STEP, SAMPLE_RING_SLOTS)
    y_sample = _merge(xs[None], o_s[None], sza_s[None], sga_s[None], r_s, wpa, wout,
                      SAMPLE_ROWS, False).reshape(db, s, d)
    k_sample = k_s.reshape(1, db, s, N_HEADS, HEAD_DIM)
    v_sample = v_s.reshape(1, db, s, N_HEADS, HEAD_DIM)
    logf_sample = lf_bsh[None]
    conv_sample = xl_s.reshape(db, s, LRU_WIDTH)[:, s - (CONV_WIDTH - 1):, :][None]
    h_sample = hlast_s[None]

    return (y_prompt, y_sample, k_prompt, v_prompt, logf_prompt, conv_prompt, h_prompt,
            k_sample, v_sample, logf_sample, conv_sample, h_sample)
```

```python
import functools

import jax
import jax.numpy as jnp
from jax import lax
from jax.experimental import pallas as pl
from jax.experimental.pallas import tpu as pltpu

F32 = jnp.float32
BF16 = jnp.bfloat16

D_MODEL = 1024
N_HEADS = 8
HEAD_DIM = 64
ATTN_WIDTH = N_HEADS * HEAD_DIM
LRU_WIDTH = D_MODEL
LRU_BLOCKS = 8
LRU_BLOCK = LRU_WIDTH // LRU_BLOCKS
CONV_WIDTH = 4
LRU_C = 8.0
NORM_EPS = 1e-6
ATTN_SCALE = HEAD_DIM ** -0.5
PAGE_SIZE = 128

V7X_LANES = 128
V7X_SUBLANES = 8
V7X_VMEM_LIMIT_BYTES = 56 * 1024 * 1024

HEAD_SLOT = 2 * HEAD_DIM
AUG_ONES = 3
V_SLOT = HEAD_DIM + 16
LOG2_E = 1.4426950408889634
MAX_SAFE_SHIFT = 50.0

NT_DIMS = (((1,), (1,)), ((), ()))


def _compiler_params(semantics):
    return pltpu.CompilerParams(dimension_semantics=semantics,
                                vmem_limit_bytes=V7X_VMEM_LIMIT_BYTES)


def _const_spec(shape):
    zeros = (0,) * len(shape)
    return pl.BlockSpec(shape, lambda *_: zeros, pipeline_mode=pl.Buffered(1))


def _split3(x):
    hi = x.astype(BF16).astype(F32)
    r = x - hi
    mid = r.astype(BF16).astype(F32)
    lo = (r - mid).astype(BF16).astype(F32)
    return hi, mid, lo


def _exact_dot(x, w):
    m = x.shape[0]
    parts = jnp.concatenate(_split3(x), axis=0).astype(BF16)
    y = jnp.dot(parts, w, preferred_element_type=F32)
    return y[:m] + y[m:2 * m] + y[2 * m:]


def _log_sigmoid(x):
    return jnp.minimum(x, 0.0) - jnp.log1p(jnp.exp(-jnp.abs(x)))


def _sigmoid(x):
    return jax.nn.sigmoid(x)


def _silu(x):
    return x * jax.nn.sigmoid(x)


def _rms_rows(x, gain_row):
    ms = jnp.mean(x * x, axis=-1, keepdims=True)
    return x * lax.rsqrt(ms + NORM_EPS) * gain_row


def _head_rms_rowmajor(u, gain_row, blockdiag_ones):
    ssq = jnp.dot((u * u).astype(BF16), blockdiag_ones, preferred_element_type=F32)
    return u * lax.rsqrt(ssq * (1.0 / HEAD_DIM) + NORM_EPS) * gain_row


def _inproj_prompt_kernel(x_ref, ng_ref, wq_ref, wk_ref, wv_ref, wza_ref, wf_ref,
                          wxl_ref, wzl_ref, wga_ref, wgb_ref,
                          qg_ref, kg_ref, bf_ref, shift_ref, tri_ref,
                          qt_ref, kp_ref, kt_ref, vt_ref, vtb_ref, szat_ref, lft_ref,
                          xl_ref, szl_ref, sga_ref, sgb_ref,
                          carry_ref):
    tm = x_ref.shape[1]

    @pl.when(pl.program_id(1) == 0)
    def _():
        carry_ref[...] = jnp.zeros_like(carry_ref)

    h = _rms_rows(x_ref[0], ng_ref[...]).astype(BF16)

    q_t = lax.dot_general(wq_ref[...], h, NT_DIMS, preferred_element_type=F32)
    k_t = lax.dot_general(wk_ref[...], h, NT_DIMS, preferred_element_type=F32)
    v_t = lax.dot_general(wv_ref[...], h, NT_DIMS, preferred_element_type=F32)
    za_t = lax.dot_general(wza_ref[...], h, NT_DIMS, preferred_element_type=F32)
    f_t = lax.dot_general(wf_ref[...], h, NT_DIMS, preferred_element_type=F32)

    logf_t = _log_sigmoid(f_t + bf_ref[...])
    lft_ref[0] = logf_t
    c_t = _exact_dot(logf_t, tri_ref[...]) + carry_ref[:, 0:1]
    carry_ref[...] = jnp.broadcast_to(c_t[:, tm - 1:tm], carry_ref.shape)
    c2_t = c_t * LOG2_E
    c_hi, c_mid, c_lo = _split3(c2_t)
    q_hi, q_mid, q_lo = _split3(c2_t - shift_ref[...])

    vt_ref[0] = v_t
    szat_ref[0] = _silu(za_t).astype(BF16)
    ones16 = jnp.ones((V_SLOT - HEAD_DIM, tm), F32)

    row8 = lax.broadcasted_iota(jnp.int32, (V7X_SUBLANES, tm), 0)
    ones_rows = jnp.where(row8 < AUG_ONES, 1.0, 0.0).astype(F32)
    zeros32 = jnp.zeros((HEAD_DIM - 4 * V7X_SUBLANES, tm), F32)
    kn_heads = []
    for hd in range(N_HEADS):
        rows = slice(hd * HEAD_DIM, (hd + 1) * HEAD_DIM)
        qh = q_t[rows]
        qn = qh * lax.rsqrt(jnp.mean(qh * qh, axis=0, keepdims=True) + NORM_EPS) * qg_ref[...]
        qn = qn * (ATTN_SCALE * LOG2_E)
        kh = k_t[rows]
        kn = kh * lax.rsqrt(jnp.mean(kh * kh, axis=0, keepdims=True) + NORM_EPS) * kg_ref[...]
        kt_ref[0, rows, :] = kn
        kn_heads.append(kn)
        vtb_ref[0, hd] = jnp.concatenate([v_t[rows], ones16], axis=0).astype(BF16)
        sel = jnp.where(row8 == hd, -1.0, 0.0).astype(F32)
        cq = jnp.where(row8 == 0, q_hi[hd:hd + 1],
                       jnp.where(row8 == 1, q_mid[hd:hd + 1],
                                 jnp.where(row8 == 2, q_lo[hd:hd + 1], 0.0)))
        aug = jnp.concatenate([cq, sel, sel, sel, zeros32], axis=0)
        if hd % 2 == 0:
            slot = jnp.concatenate([qn, aug], axis=0)
        else:
            slot = jnp.concatenate([aug, qn], axis=0)
        qt_ref[0, hd] = slot.astype(BF16)

    aug_k_t = jnp.concatenate([ones_rows, c_hi, c_mid, c_lo, zeros32], axis=0)
    aug_k = jnp.transpose(jnp.concatenate([aug_k_t, aug_k_t], axis=0))

    lane = lax.broadcasted_iota(jnp.int32, (tm, HEAD_SLOT), 1)
    for pair in range(N_HEADS // 2):
        kcol = jnp.transpose(jnp.concatenate(kn_heads[2 * pair:2 * pair + 2], axis=0))
        even = jnp.where(lane < HEAD_DIM, kcol, aug_k)
        odd = jnp.where(lane < HEAD_DIM, aug_k, kcol)
        kp_ref[0, :, (2 * pair) * HEAD_SLOT:(2 * pair + 1) * HEAD_SLOT] = even.astype(BF16)
        kp_ref[0, :, (2 * pair + 1) * HEAD_SLOT:(2 * pair + 2) * HEAD_SLOT] = odd.astype(BF16)

    xl_ref[0] = lax.dot_general(h, wxl_ref[...], NT_DIMS, preferred_element_type=F32)
    szl_ref[0] = _silu(lax.dot_general(h, wzl_ref[...], NT_DIMS,
                                       preferred_element_type=F32)).astype(BF16)
    sga_ref[0] = _sigmoid(lax.dot_general(h, wga_ref[...], NT_DIMS,
                                          preferred_element_type=F32)).astype(BF16)
    sgb_ref[0] = _sigmoid(lax.dot_general(h, wgb_ref[...], NT_DIMS,
                                          preferred_element_type=F32)).astype(BF16)


def _inproj_prompt(x, ng, w, qg_col, kg_col, bf_col, shift_col, tm):
    b, t, _ = x.shape
    tri = jnp.triu(jnp.ones((tm, tm), F32)).astype(BF16)
    row_spec = lambda n: pl.BlockSpec((1, tm, n), lambda i, j: (i, j, 0))
    col_spec = lambda n: pl.BlockSpec((1, n, tm), lambda i, j: (i, 0, j))
    out_shape = (
        jax.ShapeDtypeStruct((b, N_HEADS, HEAD_SLOT, t), BF16),
        jax.ShapeDtypeStruct((b, t, N_HEADS * HEAD_SLOT), BF16),
        jax.ShapeDtypeStruct((b, ATTN_WIDTH, t), F32),
        jax.ShapeDtypeStruct((b, ATTN_WIDTH, t), F32),
        jax.ShapeDtypeStruct((b, N_HEADS, V_SLOT, t), BF16),
        jax.ShapeDtypeStruct((b, ATTN_WIDTH, t), BF16),
        jax.ShapeDtypeStruct((b, N_HEADS, t), F32),
        jax.ShapeDtypeStruct((b, t, LRU_WIDTH), F32),
        jax.ShapeDtypeStruct((b, t, LRU_WIDTH), BF16),
        jax.ShapeDtypeStruct((b, t, D_MODEL), BF16),
        jax.ShapeDtypeStruct((b, t, D_MODEL), BF16),
    )
    out_specs = (
        pl.BlockSpec((1, N_HEADS, HEAD_SLOT, tm), lambda i, j: (i, 0, 0, j)),
        row_spec(N_HEADS * HEAD_SLOT),
        col_spec(ATTN_WIDTH), col_spec(ATTN_WIDTH),
        pl.BlockSpec((1, N_HEADS, V_SLOT, tm), lambda i, j: (i, 0, 0, j)),
        col_spec(ATTN_WIDTH), col_spec(N_HEADS),
        row_spec(LRU_WIDTH), row_spec(LRU_WIDTH), row_spec(D_MODEL), row_spec(D_MODEL),
    )
    consts = (ng, w["q"], w["k"], w["v"], w["za"], w["f"], w["xl"], w["zl"], w["ga"], w["gb"],
              qg_col, kg_col, bf_col, shift_col, tri)
    return pl.pallas_call(
        _inproj_prompt_kernel,
        grid=(b, t // tm),
        in_specs=[row_spec(D_MODEL)] + [_const_spec(c.shape) for c in consts],
        out_specs=out_specs,
        out_shape=out_shape,
        scratch_shapes=[pltpu.VMEM((N_HEADS, V7X_LANES), F32)],
        compiler_params=_compiler_params(("arbitrary", "arbitrary")),
        name="inproj_prompt",
    )(x, *consts)


def _lru_gates(xc, wri_ref, br_row, bi_row, log_sig_lam_row):
    xb = xc.astype(BF16)
    pre_r, pre_i = [], []
    for g in range(LRU_BLOCKS):
        cols = slice(g * LRU_BLOCK, (g + 1) * LRU_BLOCK)
        ri = jnp.dot(xb[:, cols], wri_ref[g], preferred_element_type=F32)
        pre_r.append(ri[:, :LRU_BLOCK])
        pre_i.append(ri[:, LRU_BLOCK:])
    r = _sigmoid(jnp.concatenate(pre_r, axis=1) + br_row)
    i = _sigmoid(jnp.concatenate(pre_i, axis=1) + bi_row)
    log_a = LRU_C * r * log_sig_lam_row
    a = jnp.exp(log_a)
    b = jnp.sqrt(-jnp.tanh(log_a) * (1.0 + a * a)) * (i * xc)
    return a, b


def _lru_prompt_kernel(xl_ref, szl_ref, sgb_ref, cw_ref, cb_ref, wri_ref, br_ref, bi_ref,
                       lam_ref, wpl_ref,
                       r_ref, hlast_ref,
                       xprev_ref, hprev_ref, a_ref, b_ref, h_ref):
    tl = xl_ref.shape[1]
    nblk = tl // V7X_SUBLANES

    @pl.when(pl.program_id(1) == 0)
    def _():
        xprev_ref[...] = jnp.zeros_like(xprev_ref)
        hprev_ref[...] = jnp.zeros_like(hprev_ref)

    x = xl_ref[0]
    prev = xprev_ref[...]
    row8 = lax.broadcasted_iota(jnp.int32, (V7X_SUBLANES, LRU_WIDTH), 0)
    xc = cb_ref[...] + x * cw_ref[CONV_WIDTH - 1:CONV_WIDTH, :]
    for back in range(1, CONV_WIDTH):
        w_row = cw_ref[CONV_WIDTH - 1 - back:CONV_WIDTH - back, :]
        rolled = pltpu.roll(x, back, 0)
        head = jnp.where(row8 < back, pltpu.roll(prev, back, 0), rolled[:V7X_SUBLANES])
        shifted = jnp.concatenate([head, rolled[V7X_SUBLANES:]], axis=0)
        xc = xc + shifted * w_row
    xprev_ref[...] = x[tl - V7X_SUBLANES:]

    a, b = _lru_gates(xc, wri_ref, br_ref[...], bi_ref[...], _log_sigmoid(lam_ref[...]))
    a_ref[...] = a
    b_ref[...] = b

    def block(k, hprev):
        rows = pl.ds(pl.multiple_of(k * V7X_SUBLANES, V7X_SUBLANES), V7X_SUBLANES)
        ak = a_ref[rows, :]
        bk = b_ref[rows, :]
        for s in (1, 2, 4):
            keep = row8 >= s
            b_sh = pltpu.roll(bk, s, 0)
            a_sh = pltpu.roll(ak, s, 0)
            bk = jnp.where(keep, ak * b_sh + bk, bk)
            ak = jnp.where(keep, ak * a_sh, ak)
        hk = ak * hprev + bk
        h_ref[rows, :] = hk
        return jnp.broadcast_to(hk[V7X_SUBLANES - 1:], hk.shape)

    hlast = lax.fori_loop(0, nblk, block, hprev_ref[...])
    hprev_ref[...] = hlast
    hlast_ref[0] = hlast

    g = (h_ref[...] * szl_ref[0].astype(F32)).astype(BF16)
    r_ref[0] = sgb_ref[0].astype(F32) * jnp.dot(g, wpl_ref[...], preferred_element_type=F32)


def _lru_prompt(xl, szl, sgb, cw, cb, wri, br, bi, lam, wpl, tl):
    b, t, c = xl.shape
    row_spec = pl.BlockSpec((1, tl, c), lambda i, j: (i, j, 0))
    consts = (cw, cb, wri, br, bi, lam, wpl)
    return pl.pallas_call(
        _lru_prompt_kernel,
        grid=(b, t // tl),
        in_specs=[row_spec, row_spec, row_spec] + [_const_spec(x.shape) for x in consts],
        out_specs=(row_spec, pl.BlockSpec((1, V7X_SUBLANES, c), lambda i, j: (i, 0, 0))),
        out_shape=(jax.ShapeDtypeStruct((b, t, D_MODEL), F32),
                   jax.ShapeDtypeStruct((b, V7X_SUBLANES, c), F32)),
        scratch_shapes=[pltpu.VMEM((V7X_SUBLANES, c), F32), pltpu.VMEM((V7X_SUBLANES, c), F32),
                        pltpu.VMEM((tl, c), F32), pltpu.VMEM((tl, c), F32), pltpu.VMEM((tl, c), F32)],
        compiler_params=_compiler_params(("arbitrary", "arbitrary")),
        name="lru_prompt",
    )(xl, szl, sgb, *consts)


def _lru_sample_kernel(xl_ref, conv0_ref, h0_ref, szl_ref, sgb_ref, cw_ref, cb_ref, wri_ref,
                       br_ref, bi_ref, lam_ref, wpl_ref,
                       r_ref, hlast_ref):
    s, n, c = xl_ref.shape
    slabs = [conv0_ref[j] for j in range(CONV_WIDTH - 1)] + [xl_ref[t] for t in range(s)]
    xc = []
    for t in range(s):
        acc = cb_ref[...] + slabs[t] * cw_ref[0:1, :]
        for j in range(1, CONV_WIDTH):
            acc = acc + slabs[t + j] * cw_ref[j:j + 1, :]
        xc.append(acc)
    xc = jnp.concatenate(xc, axis=0)
    a, b = _lru_gates(xc, wri_ref, br_ref[...], bi_ref[...], _log_sigmoid(lam_ref[...]))
    h = h0_ref[...]
    hs = []
    for t in range(s):
        h = a[t * n:(t + 1) * n] * h + b[t * n:(t + 1) * n]
        hs.append(h)
    hlast_ref[...] = h
    g = (jnp.concatenate(hs, axis=0) * szl_ref[...].reshape(s * n, c).astype(F32)).astype(BF16)
    r = jnp.dot(g, wpl_ref[...], preferred_element_type=F32)
    r_ref[...] = (sgb_ref[...].reshape(s * n, c).astype(F32) * r).reshape(s, n, c)


def _lru_sample(xl, conv0, h0, szl, sgb, cw, cb, wri, br, bi, lam, wpl):
    s, n, c = xl.shape
    return pl.pallas_call(
        _lru_sample_kernel,
        out_shape=(jax.ShapeDtypeStruct((s, n, D_MODEL), F32),
                   jax.ShapeDtypeStruct((n, c), F32)),
        compiler_params=pltpu.CompilerParams(vmem_limit_bytes=V7X_VMEM_LIMIT_BYTES),
        name="lru_sample",
    )(xl, conv0, h0, szl, sgb, cw, cb, wri, br, bi, lam, wpl)


def _fox_prompt_shifted_kernel(qt_ref, kp_ref, vt_ref, ot_ref, p_ref, *, n_sub):
    tq = qt_ref.shape[3]
    ts = tq // n_sub
    qi = pl.program_id(2)
    q_t = qt_ref[0, 0]
    key_pos = lax.broadcasted_iota(jnp.int32, (ts, tq), 0)
    qry_pos = lax.broadcasted_iota(jnp.int32, (ts, tq), 1)

    def probs(k0, slot, masked):
        for u in range(n_sub):
            ku = pl.multiple_of(k0 + u * ts, ts)
            s = jnp.dot(kp_ref[0, pl.ds(ku, ts), :], q_t, preferred_element_type=F32)
            if masked:
                s = jnp.where(key_pos + u * ts <= qry_pos, s, -jnp.inf)
            p_ref[slot, u * ts:(u + 1) * ts, :] = jnp.exp2(s).astype(BF16)

    def weighted(k0, slot):
        out = None
        for u in range(n_sub):
            ku = pl.multiple_of(k0 + u * ts, ts)
            pv = jnp.dot(vt_ref[0, 0, :, pl.ds(ku, ts)], p_ref[slot, u * ts:(u + 1) * ts, :],
                         preferred_element_type=F32)
            out = pv if out is None else out + pv
        return out

    diag0 = pl.multiple_of(qi * tq, tq)
    probs(diag0, 0, True)

    def body(j, carry):
        acc, prev = carry
        pv = weighted(prev, j % 2)
        k0 = pl.multiple_of(j * tq, tq)
        probs(k0, (j + 1) % 2, False)
        return acc + pv, k0

    acc, prev = lax.fori_loop(0, qi, body, (jnp.zeros((V_SLOT, tq), F32), diag0))
    acc = acc + weighted(prev, qi % 2)
    ot_ref[0] = acc[:HEAD_DIM] / acc[HEAD_DIM:HEAD_DIM + 1]


def _fox_prompt_online_kernel(qt_ref, kp_ref, vt_ref, ot_ref):
    tq = qt_ref.shape[3]
    qi = pl.program_id(2)
    q_t = qt_ref[0, 0]

    def step(k0, carry, mask):
        m, l, acc = carry
        s = jnp.dot(kp_ref[0, pl.ds(k0, tq), :], q_t, preferred_element_type=F32)
        if mask is not None:
            s = jnp.where(mask, s, -jnp.inf)
        m_new = jnp.maximum(m, jnp.max(s, axis=0, keepdims=True))
        alpha = jnp.exp2(m - m_new)
        p = jnp.exp2(s - m_new)
        l = alpha * l + jnp.sum(p, axis=0, keepdims=True)
        v_blk = vt_ref[0, 0, 0:HEAD_DIM, pl.ds(k0, tq)]
        acc = alpha * acc + jnp.dot(v_blk, p.astype(BF16), preferred_element_type=F32)
        return m_new, l, acc

    init = (jnp.full((1, tq), -jnp.inf, F32), jnp.zeros((1, tq), F32),
            jnp.zeros((HEAD_DIM, tq), F32))
    carry = lax.fori_loop(0, qi, lambda j, c: step(pl.multiple_of(j * tq, tq), c, None), init)
    key_pos = lax.broadcasted_iota(jnp.int32, (tq, tq), 0)
    qry_pos = lax.broadcasted_iota(jnp.int32, (tq, tq), 1)
    m, l, acc = step(pl.multiple_of(qi * tq, tq), carry, key_pos <= qry_pos)
    ot_ref[0] = acc / l


def _fox_prompt(qt, kp, vtb, tq, n_sub, shifted):
    b, nh, _, t = qt.shape
    if shifted:
        body = functools.partial(_fox_prompt_shifted_kernel, n_sub=n_sub)
        scratch = [pltpu.VMEM((2, tq, tq), BF16)]
    else:
        body, scratch = _fox_prompt_online_kernel, []
    return pl.pallas_call(
        body,
        grid=(b, nh, t // tq),
        in_specs=[pl.BlockSpec((1, 1, HEAD_SLOT, tq), lambda i, h, j: (i, h, 0, j)),
                  pl.BlockSpec((1, t, HEAD_SLOT), lambda i, h, j: (i, 0, h)),
                  pl.BlockSpec((1, 1, V_SLOT, t), lambda i, h, j: (i, h, 0, 0))],
        out_specs=pl.BlockSpec((1, HEAD_DIM, tq), lambda i, h, j: (i, h, j)),
        out_shape=jax.ShapeDtypeStruct((b, ATTN_WIDTH, t), F32),
        scratch_shapes=scratch,
        compiler_params=_compiler_params(("arbitrary", "arbitrary", "arbitrary")),
        name="fox_prompt_shifted" if shifted else "fox_prompt_online",
    )(qt, kp, vtb)


def _merge_kernel(x_ref, o_ref, sza_ref, sga_ref, r_ref, wpa_ref, wout_ref, y_ref, *,
                  attn_feature_major):
    g = o_ref[0] * sza_ref[0].astype(F32)
    if attn_feature_major:
        g = jnp.transpose(g)
    a = jnp.dot(g.astype(BF16), wpa_ref[...], preferred_element_type=F32)
    mixed = sga_ref[0].astype(F32) * a + r_ref[0]
    y_ref[0] = x_ref[0] + jnp.dot(mixed.astype(BF16), wout_ref[...], preferred_element_type=F32)


def _merge(x, o, sza, sga, r, wpa, wout, tm, attn_feature_major):
    b, t, _ = x.shape
    row_spec = lambda n: pl.BlockSpec((1, tm, n), lambda i, j: (i, j, 0))
    if attn_feature_major:
        attn_spec = pl.BlockSpec((1, ATTN_WIDTH, tm), lambda i, j: (i, 0, j))
    else:
        attn_spec = row_spec(ATTN_WIDTH)
    return pl.pallas_call(
        functools.partial(_merge_kernel, attn_feature_major=attn_feature_major),
        grid=(b, t // tm),
        in_specs=[row_spec(D_MODEL), attn_spec, attn_spec, row_spec(D_MODEL), row_spec(D_MODEL),
                  _const_spec(wpa.shape), _const_spec(wout.shape)],
        out_specs=row_spec(D_MODEL),
        out_shape=jax.ShapeDtypeStruct((b, t, D_MODEL), F32),
        compiler_params=_compiler_params(("arbitrary", "arbitrary")),
        name="merge_fm" if attn_feature_major else "merge_rm",
    )(x, o, sza, sga, r, wpa, wout)


def _inproj_sample_kernel(x_ref, ng_ref, wq_ref, wk_ref, wv_ref, wza_ref, wf_ref,
                          wxl_ref, wzl_ref, wga_ref, wgb_ref,
                          qgrow_ref, kgrow_ref, bfrow_ref, bd_ref,
                          q_ref, k_ref, v_ref, sza_ref, lf_ref, xl_ref, szl_ref, sga_ref, sgb_ref):
    h = _rms_rows(x_ref[...], ng_ref[...]).astype(BF16)
    proj = lambda w_ref: lax.dot_general(h, w_ref[...], NT_DIMS, preferred_element_type=F32)
    q_ref[...] = _head_rms_rowmajor(proj(wq_ref), qgrow_ref[...], bd_ref[...]) * ATTN_SCALE
    k_ref[...] = _head_rms_rowmajor(proj(wk_ref), kgrow_ref[...], bd_ref[...])
    v_ref[...] = proj(wv_ref)
    sza_ref[...] = _silu(proj(wza_ref)).astype(BF16)
    lf_ref[...] = _log_sigmoid(proj(wf_ref) + bfrow_ref[...])
    xl_ref[...] = proj(wxl_ref)
    szl_ref[...] = _silu(proj(wzl_ref)).astype(BF16)
    sga_ref[...] = _sigmoid(proj(wga_ref)).astype(BF16)
    sgb_ref[...] = _sigmoid(proj(wgb_ref)).astype(BF16)


def _inproj_sample(x, ng, w, wf_pad, qg_row, kg_row, bf_row, bd, tm):
    n, _ = x.shape
    row_spec = lambda c: pl.BlockSpec((tm, c), lambda i: (i, 0))
    consts = (ng, w["q"], w["k"], w["v"], w["za"], wf_pad, w["xl"], w["zl"], w["ga"], w["gb"],
              qg_row, kg_row, bf_row, bd)
    widths = (ATTN_WIDTH, ATTN_WIDTH, ATTN_WIDTH, ATTN_WIDTH, V7X_LANES,
              LRU_WIDTH, LRU_WIDTH, D_MODEL, D_MODEL)
    dtypes = (F32, F32, F32, BF16, F32, F32, BF16, BF16, BF16)
    return pl.pallas_call(
        _inproj_sample_kernel,
        grid=(n // tm,),
        in_specs=[row_spec(D_MODEL)] + [_const_spec(c.shape) for c in consts],
        out_specs=tuple(row_spec(c) for c in widths),
        out_shape=tuple(jax.ShapeDtypeStruct((n, c), d) for c, d in zip(widths, dtypes)),
        compiler_params=_compiler_params(("arbitrary",)),
        name="inproj_sample",
    )(x, *consts)


def _fox_sample_kernel(pt_ref, q_ref, k_ref, v_ref, lft_ref, ck_hbm, cv_hbm, cl_hbm,
                       o_ref,
                       kbuf, vbuf, lbuf, cpbuf, pre_ref, ksem, vsem, lsem, *,
                       pages_per_step, ring_slots):
    n_pages, n_seq = pt_ref.shape
    n_new = q_ref.shape[0] // n_seq
    pps = pages_per_step
    n_groups = n_pages // pps
    assert n_groups & (n_groups - 1) == 0, "page groups per sequence must be a power of two"
    group_shift = n_groups.bit_length() - 1
    n_steps = n_seq * n_groups
    lookahead = ring_slots - 1
    rows = n_new * N_HEADS
    page_rows = n_pages * N_HEADS

    def kv_copies(b, g, slot):
        out = []
        for i in range(pps):
            page = pt_ref[g * pps + i, b]
            out.append(pltpu.make_async_copy(ck_hbm.at[page], kbuf.at[slot, i], ksem.at[slot]))
            out.append(pltpu.make_async_copy(cv_hbm.at[page], vbuf.at[slot, i], vsem.at[slot]))
        return out

    def lf_copy(b, slot, p):
        return pltpu.make_async_copy(cl_hbm.at[pt_ref[p, b]], lbuf.at[slot, p], lsem.at[slot])

    def lf_start(b, slot):
        lax.fori_loop(0, n_pages, lambda p, c: (lf_copy(b, slot, p).start(), c)[1], 0)

    def lf_wait(b, slot):
        lax.fori_loop(0, n_pages, lambda p, c: (lf_copy(b, slot, p).wait(), c)[1], 0)

    row_head = lax.broadcasted_iota(jnp.int32, (rows, ATTN_WIDTH), 0) % N_HEADS
    lane_head = lax.broadcasted_iota(jnp.int32, (rows, ATTN_WIDTH), 1) // HEAD_DIM
    head_mask = row_head == lane_head
    lane = lax.broadcasted_iota(jnp.int32, (N_HEADS, V7X_LANES), 1)
    tri = (lax.broadcasted_iota(jnp.int32, (PAGE_SIZE, PAGE_SIZE), 0)
           <= lax.broadcasted_iota(jnp.int32, (PAGE_SIZE, PAGE_SIZE), 1)).astype(BF16)

    pr = lax.broadcasted_iota(jnp.int32, (page_rows, page_rows), 0)
    pc = lax.broadcasted_iota(jnp.int32, (page_rows, page_rows), 1)
    pre_ref[...] = jnp.where(pr % N_HEADS == pc % N_HEADS,
                             jnp.where(pc // N_HEADS < pr // N_HEADS, 1.0, 0.0), 0.0).astype(BF16)

    lf_start(0, 0)
    assert lookahead <= n_groups
    for g0 in range(lookahead):
        for c in kv_copies(0, g0, g0):
            c.start()

    def per_sequence(b, _):
        lslot = b % 2
        lf_wait(b, lslot)

        @pl.when(b + 1 < n_seq)
        def _():
            lf_start(b + 1, 1 - lslot)

        lf = lbuf[lslot].reshape(page_rows, PAGE_SIZE)
        within = _exact_dot(lf, tri)
        page_tot = jnp.broadcast_to(within[:, PAGE_SIZE - 1:], within.shape)
        parts = jnp.concatenate(_split3(page_tot), axis=1).astype(BF16)
        before = jnp.dot(pre_ref[...], parts, preferred_element_type=F32)
        cp = (within + before[:, :PAGE_SIZE] + before[:, PAGE_SIZE:2 * PAGE_SIZE]
              + before[:, 2 * PAGE_SIZE:])
        total = jnp.broadcast_to(cp[page_rows - N_HEADS:, PAGE_SIZE - 1:], (N_HEADS, PAGE_SIZE))
        after = jnp.concatenate([total] * n_pages, axis=0) - cp
        cpbuf[...] = after.reshape(n_pages, N_HEADS, PAGE_SIZE)

        tok = pl.ds(pl.multiple_of(b * n_new, n_new), n_new)
        cn = lft_ref[b]
        for s in (1, 2, 4):
            cn = cn + jnp.where(lane >= s, pltpu.roll(cn, s, 1), 0.0)
        q_b = q_ref[tok, :]
        q_bd = jnp.concatenate(
            [jnp.broadcast_to(q_b[i:i + 1], (N_HEADS, ATTN_WIDTH)) for i in range(n_new)], axis=0)
        q_bd = jnp.where(head_mask, q_bd, 0.0).astype(BF16)
        cn_q = jnp.concatenate(
            [jnp.broadcast_to(cn[:, i:i + 1], (N_HEADS, V7X_LANES)) for i in range(n_new)], axis=0)

        def attend(carry, s, v_nt=None, v_nn=None):
            m, l, acc = carry
            m_new = jnp.maximum(m, jnp.max(s, axis=1, keepdims=True))
            alpha = jnp.exp(m - m_new)
            p = jnp.exp(s - m_new)
            l = alpha * l + jnp.sum(p, axis=1, keepdims=True)
            if v_nt is not None:
                pv = lax.dot_general(p.astype(BF16), v_nt, NT_DIMS, preferred_element_type=F32)
            else:
                pv = jnp.dot(p.astype(BF16), v_nn, preferred_element_type=F32)
            return m_new, l, alpha * acc + pv

        pad = jnp.zeros((V7X_LANES - n_new, ATTN_WIDTH), F32)
        k_new = jnp.concatenate([k_ref[tok, :], pad], axis=0).astype(BF16)
        v_new = jnp.concatenate([v_ref[tok, :], pad], axis=0).astype(BF16)
        s_new = lax.dot_general(q_bd, k_new, NT_DIMS, preferred_element_type=F32)
        cn_k = jnp.concatenate([cn] * n_new, axis=0)
        key_idx = lax.broadcasted_iota(jnp.int32, (rows, V7X_LANES), 1)
        qry_idx = lax.broadcasted_iota(jnp.int32, (rows, V7X_LANES), 0) // N_HEADS
        s_new = jnp.where(key_idx <= qry_idx, s_new + cn_q - cn_k, -jnp.inf)
        init = (jnp.full((rows, 1), -jnp.inf, F32), jnp.zeros((rows, 1), F32),
                jnp.zeros((rows, ATTN_WIDTH), F32))
        carry = attend(init, s_new, v_nn=v_new)

        def per_group(g, carry):
            step = b * n_groups + g
            slot = step % ring_slots
            for c in kv_copies(b, g, slot):
                c.wait()
            ahead = step + lookahead

            @pl.when(ahead < n_steps)
            def _():
                for c in kv_copies(lax.shift_right_logical(ahead, group_shift),
                                   ahead & (n_groups - 1), ahead % ring_slots):
                    c.start()

            k_t = jnp.concatenate([kbuf[slot, i] for i in range(pps)], axis=1).astype(BF16)
            v_t = jnp.concatenate([vbuf[slot, i] for i in range(pps)], axis=1).astype(BF16)
            s = jnp.dot(q_bd, k_t, preferred_element_type=F32)
            bias = jnp.concatenate([cpbuf[g * pps + i] for i in range(pps)], axis=1)
            s = s + jnp.concatenate([cn_q] * pps, axis=1) + jnp.concatenate([bias] * n_new, axis=0)
            return attend(carry, s, v_nt=v_t)

        m, l, acc = lax.fori_loop(0, n_groups, per_group, carry)
        out = jnp.where(head_mask, acc / l, 0.0).reshape(n_new, N_HEADS, ATTN_WIDTH)
        o_ref[tok, :] = jnp.sum(out, axis=1)
        return 0

    lax.fori_loop(0, n_seq, per_sequence, 0)


def _fox_sample(pt_t, q, k, v, lf_t, cache_kt, cache_vt, cache_lt, pages_per_step, ring_slots):
    n_pages, n_seq = pt_t.shape
    vmem = pl.BlockSpec(memory_space=pltpu.VMEM)
    hbm = pl.BlockSpec(memory_space=pl.ANY)
    return pl.pallas_call(
        functools.partial(_fox_sample_kernel, pages_per_step=pages_per_step, ring_slots=ring_slots),
        in_specs=[pl.BlockSpec(memory_space=pltpu.SMEM), vmem, vmem, vmem, vmem, hbm, hbm, hbm],
        out_specs=vmem,
        out_shape=jax.ShapeDtypeStruct(q.shape, F32),
        scratch_shapes=[
            pltpu.VMEM((ring_slots, pages_per_step, ATTN_WIDTH, PAGE_SIZE), F32),
            pltpu.VMEM((ring_slots, pages_per_step, ATTN_WIDTH, PAGE_SIZE), F32),
            pltpu.VMEM((2, n_pages, N_HEADS, PAGE_SIZE), F32),
            pltpu.VMEM((n_pages, N_HEADS, PAGE_SIZE), F32),
            pltpu.VMEM((n_pages * N_HEADS, n_pages * N_HEADS), BF16),
            pltpu.SemaphoreType.DMA((ring_slots,)), pltpu.SemaphoreType.DMA((ring_slots,)),
            pltpu.SemaphoreType.DMA((2,)),
        ],
        compiler_params=pltpu.CompilerParams(vmem_limit_bytes=V7X_VMEM_LIMIT_BYTES),
        name="fox_sample",
    )(pt_t, q, k, v, lf_t, cache_kt, cache_vt, cache_lt)


INPROJ_ROWS = 512
LRU_ROWS = 256
ATTN_Q_ROWS = 1024
ATTN_SUB_BLOCKS = 4
MERGE_ROWS = 512
SAMPLE_ROWS = 256
SAMPLE_PAGES_PER_STEP = 8
SAMPLE_RING_SLOTS = 4

IN_SPLIT_NAMES = ("q", "k", "v", "za", "f", "xl", "zl", "ga", "gb")
IN_SPLIT_SIZES = (ATTN_WIDTH, ATTN_WIDTH, ATTN_WIDTH, ATTN_WIDTH, N_HEADS,
                  LRU_WIDTH, LRU_WIDTH, D_MODEL, D_MODEL)


def kernel(x_prompt, x_sample, cache_k, cache_v, cache_logf, state_conv, state_h, page_table, norm_gain, w_in, q_norm_gain, k_norm_gain, b_forget, conv_w, conv_b, w_rec_gate, b_rec_gate, w_in_gate, b_in_gate, lru_lambda, w_proj_attn, w_proj_lru, w_out):
    assert w_in.shape[0] == 1, "single-layer step"
    b, t, d = x_prompt.shape
    db, s, _ = x_sample.shape
    n_pool = cache_k.shape[1]

    wt = jnp.transpose(w_in[0]).astype(BF16)
    w, off = {}, 0
    for name, size in zip(IN_SPLIT_NAMES, IN_SPLIT_SIZES):
        w[name] = wt[off:off + size]
        off += size
    wf_pad = jnp.pad(w["f"], ((0, V7X_LANES - N_HEADS), (0, 0)))
    ng = norm_gain[0][None, :]
    qg, kg, bfg = q_norm_gain[0], k_norm_gain[0], b_forget[0]
    qg_row, kg_row = jnp.tile(qg, N_HEADS)[None, :], jnp.tile(kg, N_HEADS)[None, :]
    bf_row = jnp.pad(bfg, (0, V7X_LANES - N_HEADS))[None, :]
    head_of = jnp.arange(ATTN_WIDTH) // HEAD_DIM
    bd = (head_of[:, None] == head_of[None, :]).astype(BF16)
    wri = jnp.concatenate([w_rec_gate[0], w_in_gate[0]], axis=-1).astype(BF16)
    lru_consts = (conv_w[0], conv_b[0][None, :], wri, b_rec_gate[0][None, :],
                  b_in_gate[0][None, :], lru_lambda[0][None, :], w_proj_lru[0].astype(BF16))
    wpa, wout = w_proj_attn[0].astype(BF16), w_out[0].astype(BF16)

    score_bound = (LOG2_E * ATTN_SCALE * HEAD_DIM) * jnp.max(jnp.abs(qg)) * jnp.max(jnp.abs(kg))
    shift_is_safe = score_bound <= MAX_SAFE_SHIFT
    shift_col = jnp.broadcast_to(jnp.where(shift_is_safe, score_bound, 0.0), (N_HEADS, 1))
    qt, kp, kt, vt, vtb, szat, lft, xl, szl, sga, sgb = _inproj_prompt(
        x_prompt, ng, w, qg[:, None], kg[:, None], bfg[:, None], shift_col, INPROJ_ROWS)
    r_p, hlast_p = _lru_prompt(xl, szl, sgb, *lru_consts, LRU_ROWS)
    ot = lax.cond(shift_is_safe,
                  lambda: _fox_prompt(qt, kp, vtb, ATTN_Q_ROWS, ATTN_SUB_BLOCKS, True),
                  lambda: _fox_prompt(qt, kp, vtb, ATTN_Q_ROWS, ATTN_SUB_BLOCKS, False))
    y_prompt = _merge(x_prompt, ot, szat, sga, r_p, wpa, wout, MERGE_ROWS, True)
    to_tokens = lambda a: jnp.transpose(a.reshape(b, N_HEADS, HEAD_DIM, t), (0, 3, 1, 2))[None]
    k_prompt, v_prompt = to_tokens(kt), to_tokens(vt)
    logf_prompt = jnp.transpose(lft, (0, 2, 1))[None]
    conv_prompt = xl[:, t - (CONV_WIDTH - 1):, :][None]
    h_prompt = hlast_p[:, 0, :][None]

    xs = x_sample.reshape(db * s, d)
    q_s, k_s, v_s, sza_s, lf_s, xl_s, szl_s, sga_s, sgb_s = _inproj_sample(
        xs, ng, w, wf_pad, qg_row, kg_row, bf_row, bd, SAMPLE_ROWS)
    time_major = lambda a: jnp.transpose(a.reshape(db, s, -1), (1, 0, 2))
    conv0 = jnp.transpose(state_conv[0], (1, 0, 2))
    r_tm, hlast_s = _lru_sample(time_major(xl_s), conv0, state_h[0], time_major(szl_s),
                                time_major(sgb_s), *lru_consts)
    r_s = jnp.transpose(r_tm, (1, 0, 2)).reshape(1, db * s, d)
    lf_bsh = lf_s[:, :N_HEADS].reshape(db, s, N_HEADS)
    lf_t = jnp.pad(jnp.transpose(lf_bsh, (0, 2, 1)), ((0, 0), (0, 0), (0, V7X_LANES - s)))
    cache_kt = jnp.transpose(cache_k[0], (0, 2, 3, 1)).reshape(n_pool, ATTN_WIDTH, PAGE_SIZE)
    cache_vt = jnp.transpose(cache_v[0], (0, 2, 3, 1)).reshape(n_pool, ATTN_WIDTH, PAGE_SIZE)
    cache_lt = jnp.transpose(cache_logf[0], (0, 2, 1))
    o_s = _fox_sample(jnp.transpose(page_table), q_s, k_s, v_s, lf_t, cache_kt, cache_vt,
                      cache_lt, SAMPLE_PAGES_PER_STEP, SAMPLE_RING_SLOTS)
    y_sample = _merge(xs[None], o_s[None], sza_s[None], sga_s[None], r_s, wpa, wout,
                      SAMPLE_ROWS, False).reshape(db, s, d)
    k_sample = k_s.reshape(1, db, s, N_HEADS, HEAD_DIM)
    v_sample = v_s.reshape(1, db, s, N_HEADS, HEAD_DIM)
    logf_sample = lf_bsh[None]
    conv_sample = xl_s.reshape(db, s, LRU_WIDTH)[:, s - (CONV_WIDTH - 1):, :][None]
    h_sample = hlast_s[None]

    return (y_prompt, y_sample, k_prompt, v_prompt, logf_prompt, conv_prompt, h_prompt,
            k_sample, v_sample, logf_sample, conv_sample, h_sample)
```

```python
import functools

import jax
import jax.numpy as jnp
from jax import lax
from jax.experimental import pallas as pl
from jax.experimental.pallas import tpu as pltpu

F32 = jnp.float32
BF16 = jnp.bfloat16

D_MODEL = 1024
N_HEADS = 8
HEAD_DIM = 64
ATTN_WIDTH = N_HEADS * HEAD_DIM
LRU_WIDTH = D_MODEL
LRU_BLOCKS = 8
LRU_BLOCK = LRU_WIDTH // LRU_BLOCKS
CONV_WIDTH = 4
LRU_C = 8.0
NORM_EPS = 1e-6
ATTN_SCALE = HEAD_DIM ** -0.5
PAGE_SIZE = 128

V7X_LANES = 128
V7X_SUBLANES = 8
V7X_VMEM_LIMIT_BYTES = 56 * 1024 * 1024

HEAD_SLOT = 2 * HEAD_DIM
AUG_ONES = 3
V_SLOT = HEAD_DIM + 16
LOG2_E = 1.4426950408889634
MAX_SAFE_SHIFT = 50.0

NT_DIMS = (((1,), (1,)), ((), ()))


def _compiler_params(semantics):
    return pltpu.CompilerParams(dimension_semantics=semantics,
                                vmem_limit_bytes=V7X_VMEM_LIMIT_BYTES)


def _const_spec(shape):
    zeros = (0,) * len(shape)
    return pl.BlockSpec(shape, lambda *_: zeros, pipeline_mode=pl.Buffered(1))


def _split3(x):
    hi = x.astype(BF16).astype(F32)
    r = x - hi
    mid = r.astype(BF16).astype(F32)
    lo = (r - mid).astype(BF16).astype(F32)
    return hi, mid, lo


def _exact_dot(x, w):
    m = x.shape[0]
    parts = jnp.concatenate(_split3(x), axis=0).astype(BF16)
    y = jnp.dot(parts, w, preferred_element_type=F32)
    return y[:m] + y[m:2 * m] + y[2 * m:]


def _log_sigmoid(x):
    return jnp.minimum(x, 0.0) - jnp.log1p(jnp.exp(-jnp.abs(x)))


def _sigmoid(x):
    return jax.nn.sigmoid(x)


def _silu(x):
    return x * jax.nn.sigmoid(x)


def _rms_rows(x, gain_row):
    ms = jnp.mean(x * x, axis=-1, keepdims=True)
    return x * lax.rsqrt(ms + NORM_EPS) * gain_row


def _head_rms_rowmajor(u, gain_row, blockdiag_ones):
    ssq = jnp.dot((u * u).astype(BF16), blockdiag_ones, preferred_element_type=F32)
    return u * lax.rsqrt(ssq * (1.0 / HEAD_DIM) + NORM_EPS) * gain_row


def _inproj_prompt_kernel(x_ref, ng_ref, wq_ref, wk_ref, wv_ref, wza_ref, wf_ref,
                          wxl_ref, wzl_ref, wga_ref, wgb_ref,
                          qg_ref, kg_ref, bf_ref, shift_ref, tri_ref,
                          qt_ref, kp_ref, kt_ref, vt_ref, vtb_ref, szat_ref, lft_ref,
                          xl_ref, szl_ref, sga_ref, sgb_ref,
                          carry_ref):
    tm = x_ref.shape[1]

    @pl.when(pl.program_id(1) == 0)
    def _():
        carry_ref[...] = jnp.zeros_like(carry_ref)

    h = _rms_rows(x_ref[0], ng_ref[...]).astype(BF16)

    q_t = lax.dot_general(wq_ref[...], h, NT_DIMS, preferred_element_type=F32)
    k_t = lax.dot_general(wk_ref[...], h, NT_DIMS, preferred_element_type=F32)
    v_t = lax.dot_general(wv_ref[...], h, NT_DIMS, preferred_element_type=F32)
    za_t = lax.dot_general(wza_ref[...], h, NT_DIMS, preferred_element_type=F32)
    f_t = lax.dot_general(wf_ref[...], h, NT_DIMS, preferred_element_type=F32)

    logf_t = _log_sigmoid(f_t + bf_ref[...])
    lft_ref[0] = logf_t
    c_t = _exact_dot(logf_t, tri_ref[...]) + carry_ref[:, 0:1]
    carry_ref[...] = jnp.broadcast_to(c_t[:, tm - 1:tm], carry_ref.shape)
    c2_t = c_t * LOG2_E
    c_hi, c_mid, c_lo = _split3(c2_t)
    q_hi, q_mid, q_lo = _split3(c2_t - shift_ref[...])

    vt_ref[0] = v_t
    szat_ref[0] = _silu(za_t).astype(BF16)
    ones16 = jnp.ones((V_SLOT - HEAD_DIM, tm), F32)

    row8 = lax.broadcasted_iota(jnp.int32, (V7X_SUBLANES, tm), 0)
    ones_rows = jnp.where(row8 < AUG_ONES, 1.0, 0.0).astype(F32)
    zeros32 = jnp.zeros((HEAD_DIM - 4 * V7X_SUBLANES, tm), F32)
    kn_heads = []
    for hd in range(N_HEADS):
        rows = slice(hd * HEAD_DIM, (hd + 1) * HEAD_DIM)
        qh = q_t[rows]
        qn = qh * lax.rsqrt(jnp.mean(qh * qh, axis=0, keepdims=True) + NORM_EPS) * qg_ref[...]
        qn = qn * (ATTN_SCALE * LOG2_E)
        kh = k_t[rows]
        kn = kh * lax.rsqrt(jnp.mean(kh * kh, axis=0, keepdims=True) + NORM_EPS) * kg_ref[...]
        kt_ref[0, rows, :] = kn
        kn_heads.append(kn)
        vtb_ref[0, hd] = jnp.concatenate([v_t[rows], ones16], axis=0).astype(BF16)
        sel = jnp.where(row8 == hd, -1.0, 0.0).astype(F32)
        cq = jnp.where(row8 == 0, q_hi[hd:hd + 1],
                       jnp.where(row8 == 1, q_mid[hd:hd + 1],
                                 jnp.where(row8 == 2, q_lo[hd:hd + 1], 0.0)))
        aug = jnp.concatenate([cq, sel, sel, sel, zeros32], axis=0)
        if hd % 2 == 0:
            slot = jnp.concatenate([qn, aug], axis=0)
        else:
            slot = jnp.concatenate([aug, qn], axis=0)
        qt_ref[0, hd] = slot.astype(BF16)

    aug_k_t = jnp.concatenate([ones_rows, c_hi, c_mid, c_lo, zeros32], axis=0)
    aug_k = jnp.transpose(jnp.concatenate([aug_k_t, aug_k_t], axis=0))

    lane = lax.broadcasted_iota(jnp.int32, (tm, HEAD_SLOT), 1)
    for pair in range(N_HEADS // 2):
        kcol = jnp.transpose(jnp.concatenate(kn_heads[2 * pair:2 * pair + 2], axis=0))
        even = jnp.where(lane < HEAD_DIM, kcol, aug_k)
        odd = jnp.where(lane < HEAD_DIM, aug_k, kcol)
        kp_ref[0, :, (2 * pair) * HEAD_SLOT:(2 * pair + 1) * HEAD_SLOT] = even.astype(BF16)
        kp_ref[0, :, (2 * pair + 1) * HEAD_SLOT:(2 * pair + 2) * HEAD_SLOT] = odd.astype(BF16)

    xl_ref[0] = lax.dot_general(h, wxl_ref[...], NT_DIMS, preferred_element_type=F32)
    szl_ref[0] = _silu(lax.dot_general(h, wzl_ref[...], NT_DIMS,
                                       preferred_element_type=F32)).astype(BF16)
    sga_ref[0] = _sigmoid(lax.dot_general(h, wga_ref[...], NT_DIMS,
                                          preferred_element_type=F32)).astype(BF16)
    sgb_ref[0] = _sigmoid(lax.dot_general(h, wgb_ref[...], NT_DIMS,
                                          preferred_element_type=F32)).astype(BF16)


def _inproj_prompt(x, ng, w, qg_col, kg_col, bf_col, shift_col, tm):
    b, t, _ = x.shape
    tri = jnp.triu(jnp.ones((tm, tm), F32)).astype(BF16)
    row_spec = lambda n: pl.BlockSpec((1, tm, n), lambda i, j: (i, j, 0))
    col_spec = lambda n: pl.BlockSpec((1, n, tm), lambda i, j: (i, 0, j))
    out_shape = (
        jax.ShapeDtypeStruct((b, N_HEADS, HEAD_SLOT, t), BF16),
        jax.ShapeDtypeStruct((b, t, N_HEADS * HEAD_SLOT), BF16),
        jax.ShapeDtypeStruct((b, ATTN_WIDTH, t), F32),
        jax.ShapeDtypeStruct((b, ATTN_WIDTH, t), F32),
        jax.ShapeDtypeStruct((b, N_HEADS, V_SLOT, t), BF16),
        jax.ShapeDtypeStruct((b, ATTN_WIDTH, t), BF16),
        jax.ShapeDtypeStruct((b, N_HEADS, t), F32),
        jax.ShapeDtypeStruct((b, t, LRU_WIDTH), F32),
        jax.ShapeDtypeStruct((b, t, LRU_WIDTH), BF16),
        jax.ShapeDtypeStruct((b, t, D_MODEL), BF16),
        jax.ShapeDtypeStruct((b, t, D_MODEL), BF16),
    )
    out_specs = (
        pl.BlockSpec((1, N_HEADS, HEAD_SLOT, tm), lambda i, j: (i, 0, 0, j)),
        row_spec(N_HEADS * HEAD_SLOT),
        col_spec(ATTN_WIDTH), col_spec(ATTN_WIDTH),
        pl.BlockSpec((1, N_HEADS, V_SLOT, tm), lambda i, j: (i, 0, 0, j)),
        col_spec(ATTN_WIDTH), col_spec(N_HEADS),
        row_spec(LRU_WIDTH), row_spec(LRU_WIDTH), row_spec(D_MODEL), row_spec(D_MODEL),
    )
    consts = (ng, w["q"], w["k"], w["v"], w["za"], w["f"], w["xl"], w["zl"], w["ga"], w["gb"],
              qg_col, kg_col, bf_col, shift_col, tri)
    return pl.pallas_call(
        _inproj_prompt_kernel,
        grid=(b, t // tm),
        in_specs=[row_spec(D_MODEL)] + [_const_spec(c.shape) for c in consts],
        out_specs=out_specs,
        out_shape=out_shape,
        scratch_shapes=[pltpu.VMEM((N_HEADS, V7X_LANES), F32)],
        compiler_params=_compiler_params(("arbitrary", "arbitrary")),
        name="inproj_prompt",
    )(x, *consts)


def _lru_gates(xc, wri_ref, br_row, bi_row, log_sig_lam_row):
    xb = xc.astype(BF16)
    pre_r, pre_i = [], []
    for g in range(LRU_BLOCKS):
        cols = slice(g * LRU_BLOCK, (g + 1) * LRU_BLOCK)
        ri = jnp.dot(xb[:, cols], wri_ref[g], preferred_element_type=F32)
        pre_r.append(ri[:, :LRU_BLOCK])
        pre_i.append(ri[:, LRU_BLOCK:])
    r = _sigmoid(jnp.concatenate(pre_r, axis=1) + br_row)
    i = _sigmoid(jnp.concatenate(pre_i, axis=1) + bi_row)
    log_a = LRU_C * r * log_sig_lam_row
    a = jnp.exp(log_a)
    b = jnp.sqrt(-jnp.tanh(log_a) * (1.0 + a * a)) * (i * xc)
    return a, b


def _lru_prompt_tile(first_tile, xl_ref, szl_ref, sgb_ref, cw_ref, cb_ref, wri_ref, br_ref, bi_ref,
                     lam_ref, wpl_ref,
                     r_ref, hlast_ref,
                     xprev_ref, hprev_ref, a_ref, b_ref, h_ref):
    tl = xl_ref.shape[1]
    nblk = tl // V7X_SUBLANES

    @pl.when(first_tile)
    def _():
        xprev_ref[...] = jnp.zeros_like(xprev_ref)
        hprev_ref[...] = jnp.zeros_like(hprev_ref)

    x = xl_ref[0]
    prev = xprev_ref[...]
    row8 = lax.broadcasted_iota(jnp.int32, (V7X_SUBLANES, LRU_WIDTH), 0)
    xc = cb_ref[...] + x * cw_ref[CONV_WIDTH - 1:CONV_WIDTH, :]
    for back in range(1, CONV_WIDTH):
        w_row = cw_ref[CONV_WIDTH - 1 - back:CONV_WIDTH - back, :]
        rolled = pltpu.roll(x, back, 0)
        head = jnp.where(row8 < back, pltpu.roll(prev, back, 0), rolled[:V7X_SUBLANES])
        shifted = jnp.concatenate([head, rolled[V7X_SUBLANES:]], axis=0)
        xc = xc + shifted * w_row
    xprev_ref[...] = x[tl - V7X_SUBLANES:]

    a, b = _lru_gates(xc, wri_ref, br_ref[...], bi_ref[...], _log_sigmoid(lam_ref[...]))
    a_ref[...] = a
    b_ref[...] = b

    def block(k, hprev):
        rows = pl.ds(pl.multiple_of(k * V7X_SUBLANES, V7X_SUBLANES), V7X_SUBLANES)
        ak = a_ref[rows, :]
        bk = b_ref[rows, :]
        for s in (1, 2, 4):
            keep = row8 >= s
            b_sh = pltpu.roll(bk, s, 0)
            a_sh = pltpu.roll(ak, s, 0)
            bk = jnp.where(keep, ak * b_sh + bk, bk)
            ak = jnp.where(keep, ak * a_sh, ak)
        hk = ak * hprev + bk
        h_ref[rows, :] = hk
        return jnp.broadcast_to(hk[V7X_SUBLANES - 1:], hk.shape)

    hlast = lax.fori_loop(0, nblk, block, hprev_ref[...])
    hprev_ref[...] = hlast
    hlast_ref[0] = hlast

    g = (h_ref[...] * szl_ref[0].astype(F32)).astype(BF16)
    r_ref[0] = sgb_ref[0].astype(F32) * jnp.dot(g, wpl_ref[...], preferred_element_type=F32)


def _lru_sample_kernel(xl_ref, conv0_ref, h0_ref, szl_ref, sgb_ref, cw_ref, cb_ref, wri_ref,
                       br_ref, bi_ref, lam_ref, wpl_ref,
                       r_ref, hlast_ref):
    s, n, c = xl_ref.shape
    slabs = [conv0_ref[j] for j in range(CONV_WIDTH - 1)] + [xl_ref[t] for t in range(s)]
    xc = []
    for t in range(s):
        acc = cb_ref[...] + slabs[t] * cw_ref[0:1, :]
        for j in range(1, CONV_WIDTH):
            acc = acc + slabs[t + j] * cw_ref[j:j + 1, :]
        xc.append(acc)
    xc = jnp.concatenate(xc, axis=0)
    a, b = _lru_gates(xc, wri_ref, br_ref[...], bi_ref[...], _log_sigmoid(lam_ref[...]))
    h = h0_ref[...]
    hs = []
    for t in range(s):
        h = a[t * n:(t + 1) * n] * h + b[t * n:(t + 1) * n]
        hs.append(h)
    hlast_ref[...] = h
    g = (jnp.concatenate(hs, axis=0) * szl_ref[...].reshape(s * n, c).astype(F32)).astype(BF16)
    r = jnp.dot(g, wpl_ref[...], preferred_element_type=F32)
    r_ref[...] = (sgb_ref[...].reshape(s * n, c).astype(F32) * r).reshape(s, n, c)


def _lru_sample(xl, conv0, h0, szl, sgb, cw, cb, wri, br, bi, lam, wpl):
    s, n, c = xl.shape
    return pl.pallas_call(
        _lru_sample_kernel,
        out_shape=(jax.ShapeDtypeStruct((s, n, D_MODEL), F32),
                   jax.ShapeDtypeStruct((n, c), F32)),
        compiler_params=pltpu.CompilerParams(vmem_limit_bytes=V7X_VMEM_LIMIT_BYTES),
        name="lru_sample",
    )(xl, conv0, h0, szl, sgb, cw, cb, wri, br, bi, lam, wpl)


def _fox_prompt_shifted_kernel(qt_ref, kp_ref, vt_ref, ot_ref, p_ref, *, n_sub):
    tq = qt_ref.shape[3]
    ts = tq // n_sub
    qi = pl.program_id(2)
    q_t = qt_ref[0, 0]
    key_pos = lax.broadcasted_iota(jnp.int32, (ts, tq), 0)
    qry_pos = lax.broadcasted_iota(jnp.int32, (ts, tq), 1)

    def probs(k0, slot, masked):
        for u in range(n_sub):
            ku = pl.multiple_of(k0 + u * ts, ts)
            s = jnp.dot(kp_ref[0, pl.ds(ku, ts), :], q_t, preferred_element_type=F32)
            if masked:
                s = jnp.where(key_pos + u * ts <= qry_pos, s, -jnp.inf)
            p_ref[slot, u * ts:(u + 1) * ts, :] = jnp.exp2(s).astype(BF16)

    def weighted(k0, slot):
        out = None
        for u in range(n_sub):
            ku = pl.multiple_of(k0 + u * ts, ts)
            pv = jnp.dot(vt_ref[0, 0, :, pl.ds(ku, ts)], p_ref[slot, u * ts:(u + 1) * ts, :],
                         preferred_element_type=F32)
            out = pv if out is None else out + pv
        return out

    diag0 = pl.multiple_of(qi * tq, tq)
    probs(diag0, 0, True)

    def body(j, carry):
        acc, prev = carry
        pv = weighted(prev, j % 2)
        k0 = pl.multiple_of(j * tq, tq)
        probs(k0, (j + 1) % 2, False)
        return acc + pv, k0

    acc, prev = lax.fori_loop(0, qi, body, (jnp.zeros((V_SLOT, tq), F32), diag0))
    acc = acc + weighted(prev, qi % 2)
    ot_ref[0] = acc[:HEAD_DIM] / acc[HEAD_DIM:HEAD_DIM + 1]


def _fox_prompt_online_kernel(qt_ref, kp_ref, vt_ref, ot_ref):
    tq = qt_ref.shape[3]
    qi = pl.program_id(2)
    q_t = qt_ref[0, 0]

    def step(k0, carry, mask):
        m, l, acc = carry
        s = jnp.dot(kp_ref[0, pl.ds(k0, tq), :], q_t, preferred_element_type=F32)
        if mask is not None:
            s = jnp.where(mask, s, -jnp.inf)
        m_new = jnp.maximum(m, jnp.max(s, axis=0, keepdims=True))
        alpha = jnp.exp2(m - m_new)
        p = jnp.exp2(s - m_new)
        l = alpha * l + jnp.sum(p, axis=0, keepdims=True)
        v_blk = vt_ref[0, 0, 0:HEAD_DIM, pl.ds(k0, tq)]
        acc = alpha * acc + jnp.dot(v_blk, p.astype(BF16), preferred_element_type=F32)
        return m_new, l, acc

    init = (jnp.full((1, tq), -jnp.inf, F32), jnp.zeros((1, tq), F32),
            jnp.zeros((HEAD_DIM, tq), F32))
    carry = lax.fori_loop(0, qi, lambda j, c: step(pl.multiple_of(j * tq, tq), c, None), init)
    key_pos = lax.broadcasted_iota(jnp.int32, (tq, tq), 0)
    qry_pos = lax.broadcasted_iota(jnp.int32, (tq, tq), 1)
    m, l, acc = step(pl.multiple_of(qi * tq, tq), carry, key_pos <= qry_pos)
    ot_ref[0] = acc / l


def _fox_prompt(qt, kp, vtb, tq, n_sub, shifted):
    b, nh, _, t = qt.shape
    if shifted:
        body = functools.partial(_fox_prompt_shifted_kernel, n_sub=n_sub)
        scratch = [pltpu.VMEM((2, tq, tq), BF16)]
    else:
        body, scratch = _fox_prompt_online_kernel, []
    return pl.pallas_call(
        body,
        grid=(b, nh, t // tq),
        in_specs=[pl.BlockSpec((1, 1, HEAD_SLOT, tq), lambda i, h, j: (i, h, 0, j)),
                  pl.BlockSpec((1, t, HEAD_SLOT), lambda i, h, j: (i, 0, h)),
                  pl.BlockSpec((1, 1, V_SLOT, t), lambda i, h, j: (i, h, 0, 0))],
        out_specs=pl.BlockSpec((1, HEAD_DIM, tq), lambda i, h, j: (i, h, j)),
        out_shape=jax.ShapeDtypeStruct((b, ATTN_WIDTH, t), F32),
        scratch_shapes=scratch,
        compiler_params=_compiler_params(("arbitrary", "arbitrary", "arbitrary")),
        name="fox_prompt_shifted" if shifted else "fox_prompt_online",
    )(qt, kp, vtb)


def _merge_kernel(x_ref, o_ref, sza_ref, sga_ref, r_ref, wpa_ref, wout_ref, y_ref, *,
                  attn_feature_major):
    g = o_ref[0] * sza_ref[0].astype(F32)
    if attn_feature_major:
        g = jnp.transpose(g)
    a = jnp.dot(g.astype(BF16), wpa_ref[...], preferred_element_type=F32)
    mixed = sga_ref[0].astype(F32) * a + r_ref[0]
    y_ref[0] = x_ref[0] + jnp.dot(mixed.astype(BF16), wout_ref[...], preferred_element_type=F32)


def _merge(x, o, sza, sga, r, wpa, wout, tm, attn_feature_major):
    b, t, _ = x.shape
    row_spec = lambda n: pl.BlockSpec((1, tm, n), lambda i, j: (i, j, 0))
    if attn_feature_major:
        attn_spec = pl.BlockSpec((1, ATTN_WIDTH, tm), lambda i, j: (i, 0, j))
    else:
        attn_spec = row_spec(ATTN_WIDTH)
    return pl.pallas_call(
        functools.partial(_merge_kernel, attn_feature_major=attn_feature_major),
        grid=(b, t // tm),
        in_specs=[row_spec(D_MODEL), attn_spec, attn_spec, row_spec(D_MODEL), row_spec(D_MODEL),
                  _const_spec(wpa.shape), _const_spec(wout.shape)],
        out_specs=row_spec(D_MODEL),
        out_shape=jax.ShapeDtypeStruct((b, t, D_MODEL), F32),
        compiler_params=_compiler_params(("arbitrary", "arbitrary")),
        name="merge_fm" if attn_feature_major else "merge_rm",
    )(x, o, sza, sga, r, wpa, wout)


def _inproj_sample_kernel(x_ref, ng_ref, wq_ref, wk_ref, wv_ref, wza_ref, wf_ref,
                          wxl_ref, wzl_ref, wga_ref, wgb_ref,
                          qgrow_ref, kgrow_ref, bfrow_ref, bd_ref,
                          q_ref, k_ref, v_ref, sza_ref, lf_ref, xl_ref, szl_ref, sga_ref, sgb_ref):
    h = _rms_rows(x_ref[...], ng_ref[...]).astype(BF16)
    proj = lambda w_ref: lax.dot_general(h, w_ref[...], NT_DIMS, preferred_element_type=F32)
    q_ref[...] = _head_rms_rowmajor(proj(wq_ref), qgrow_ref[...], bd_ref[...]) * ATTN_SCALE
    k_ref[...] = _head_rms_rowmajor(proj(wk_ref), kgrow_ref[...], bd_ref[...])
    v_ref[...] = proj(wv_ref)
    sza_ref[...] = _silu(proj(wza_ref)).astype(BF16)
    lf_ref[...] = _log_sigmoid(proj(wf_ref) + bfrow_ref[...])
    xl_ref[...] = proj(wxl_ref)
    szl_ref[...] = _silu(proj(wzl_ref)).astype(BF16)
    sga_ref[...] = _sigmoid(proj(wga_ref)).astype(BF16)
    sgb_ref[...] = _sigmoid(proj(wgb_ref)).astype(BF16)


def _inproj_sample(x, ng, w, wf_pad, qg_row, kg_row, bf_row, bd, tm):
    n, _ = x.shape
    row_spec = lambda c: pl.BlockSpec((tm, c), lambda i: (i, 0))
    consts = (ng, w["q"], w["k"], w["v"], w["za"], wf_pad, w["xl"], w["zl"], w["ga"], w["gb"],
              qg_row, kg_row, bf_row, bd)
    widths = (ATTN_WIDTH, ATTN_WIDTH, ATTN_WIDTH, ATTN_WIDTH, V7X_LANES,
              LRU_WIDTH, LRU_WIDTH, D_MODEL, D_MODEL)
    dtypes = (F32, F32, F32, BF16, F32, F32, BF16, BF16, BF16)
    return pl.pallas_call(
        _inproj_sample_kernel,
        grid=(n // tm,),
        in_specs=[row_spec(D_MODEL)] + [_const_spec(c.shape) for c in consts],
        out_specs=tuple(row_spec(c) for c in widths),
        out_shape=tuple(jax.ShapeDtypeStruct((n, c), d) for c, d in zip(widths, dtypes)),
        compiler_params=_compiler_params(("arbitrary",)),
        name="inproj_sample",
    )(x, *consts)


def _fox_sample_lru_prompt_kernel(pt_ref, q_ref, k_ref, v_ref, lft_ref, ck_hbm, cv_hbm, cl_hbm,
                                  xl_ref, szl_ref, sgb_ref, cw_ref, cb_ref, wri_ref, br_ref, bi_ref,
                                  lam_ref, wpl_ref,
                                  o_ref, r_ref, hlast_ref,
                                  kbuf, vbuf, lbuf, cpbuf, pre_ref, ksem, vsem, lsem,
                                  xprev_ref, hprev_ref, a_ref, b_ref, h_ref, *,
                                  pages_per_step, ring_slots, seq_per_step, tiles_per_batch):
    n_pages, n_seq = pt_ref.shape
    grid_step = pl.program_id(0)
    n_new = q_ref.shape[0] // n_seq
    pps = pages_per_step
    n_groups = n_pages // pps
    assert n_groups & (n_groups - 1) == 0, "page groups per sequence must be a power of two"
    group_shift = n_groups.bit_length() - 1
    n_steps = n_seq * n_groups
    lookahead = ring_slots - 1
    rows = n_new * N_HEADS
    page_rows = n_pages * N_HEADS

    def kv_copies(b, g, slot):
        out = []
        for i in range(pps):
            page = pt_ref[g * pps + i, b]
            out.append(pltpu.make_async_copy(ck_hbm.at[page], kbuf.at[slot, i], ksem.at[slot]))
            out.append(pltpu.make_async_copy(cv_hbm.at[page], vbuf.at[slot, i], vsem.at[slot]))
        return out

    def lf_copy(b, slot, p):
        return pltpu.make_async_copy(cl_hbm.at[pt_ref[p, b]], lbuf.at[slot, p], lsem.at[slot])

    def lf_start(b, slot):
        lax.fori_loop(0, n_pages, lambda p, c: (lf_copy(b, slot, p).start(), c)[1], 0)

    def lf_wait(b, slot):
        lax.fori_loop(0, n_pages, lambda p, c: (lf_copy(b, slot, p).wait(), c)[1], 0)

    row_head = lax.broadcasted_iota(jnp.int32, (rows, ATTN_WIDTH), 0) % N_HEADS
    lane_head = lax.broadcasted_iota(jnp.int32, (rows, ATTN_WIDTH), 1) // HEAD_DIM
    head_mask = row_head == lane_head
    lane = lax.broadcasted_iota(jnp.int32, (N_HEADS, V7X_LANES), 1)
    tri = (lax.broadcasted_iota(jnp.int32, (PAGE_SIZE, PAGE_SIZE), 0)
           <= lax.broadcasted_iota(jnp.int32, (PAGE_SIZE, PAGE_SIZE), 1)).astype(BF16)

    assert lookahead <= n_groups

    @pl.when(grid_step == 0)
    def _():
        pr = lax.broadcasted_iota(jnp.int32, (page_rows, page_rows), 0)
        pc = lax.broadcasted_iota(jnp.int32, (page_rows, page_rows), 1)
        pre_ref[...] = jnp.where(pr % N_HEADS == pc % N_HEADS,
                                 jnp.where(pc // N_HEADS < pr // N_HEADS, 1.0, 0.0), 0.0).astype(BF16)
        lf_start(0, 0)
        for g0 in range(lookahead):
            for c in kv_copies(0, g0, g0):
                c.start()

    def per_sequence(b):
        lslot = b % 2
        lf_wait(b, lslot)

        @pl.when(b + 1 < n_seq)
        def _():
            lf_start(b + 1, 1 - lslot)

        lf = lbuf[lslot].reshape(page_rows, PAGE_SIZE)
        within = _exact_dot(lf, tri)
        page_tot = jnp.broadcast_to(within[:, PAGE_SIZE - 1:], within.shape)
        parts = jnp.concatenate(_split3(page_tot), axis=1).astype(BF16)
        before = jnp.dot(pre_ref[...], parts, preferred_element_type=F32)
        cp = (within + before[:, :PAGE_SIZE] + before[:, PAGE_SIZE:2 * PAGE_SIZE]
              + before[:, 2 * PAGE_SIZE:])
        total = jnp.broadcast_to(cp[page_rows - N_HEADS:, PAGE_SIZE - 1:], (N_HEADS, PAGE_SIZE))
        after = jnp.concatenate([total] * n_pages, axis=0) - cp
        cpbuf[...] = after.reshape(n_pages, N_HEADS, PAGE_SIZE)

        tok = pl.ds(pl.multiple_of(b * n_new, n_new), n_new)
        cn = lft_ref[b]
        for s in (1, 2, 4):
            cn = cn + jnp.where(lane >= s, pltpu.roll(cn, s, 1), 0.0)
        q_b = q_ref[tok, :]
        q_bd = jnp.concatenate(
            [jnp.broadcast_to(q_b[i:i + 1], (N_HEADS, ATTN_WIDTH)) for i in range(n_new)], axis=0)
        q_bd = jnp.where(head_mask, q_bd, 0.0).astype(BF16)
        cn_q = jnp.concatenate(
            [jnp.broadcast_to(cn[:, i:i + 1], (N_HEADS, V7X_LANES)) for i in range(n_new)], axis=0)

        def attend(carry, s, v_nt=None, v_nn=None):
            m, l, acc = carry
            m_new = jnp.maximum(m, jnp.max(s, axis=1, keepdims=True))
            alpha = jnp.exp(m - m_new)
            p = jnp.exp(s - m_new)
            l = alpha * l + jnp.sum(p, axis=1, keepdims=True)
            if v_nt is not None:
                pv = lax.dot_general(p.astype(BF16), v_nt, NT_DIMS, preferred_element_type=F32)
            else:
                pv = jnp.dot(p.astype(BF16), v_nn, preferred_element_type=F32)
            return m_new, l, alpha * acc + pv

        pad = jnp.zeros((V7X_LANES - n_new, ATTN_WIDTH), F32)
        k_new = jnp.concatenate([k_ref[tok, :], pad], axis=0).astype(BF16)
        v_new = jnp.concatenate([v_ref[tok, :], pad], axis=0).astype(BF16)
        s_new = lax.dot_general(q_bd, k_new, NT_DIMS, preferred_element_type=F32)
        cn_k = jnp.concatenate([cn] * n_new, axis=0)
        key_idx = lax.broadcasted_iota(jnp.int32, (rows, V7X_LANES), 1)
        qry_idx = lax.broadcasted_iota(jnp.int32, (rows, V7X_LANES), 0) // N_HEADS
        s_new = jnp.where(key_idx <= qry_idx, s_new + cn_q - cn_k, -jnp.inf)
        init = (jnp.full((rows, 1), -jnp.inf, F32), jnp.zeros((rows, 1), F32),
                jnp.zeros((rows, ATTN_WIDTH), F32))
        carry = attend(init, s_new, v_nn=v_new)

        def per_group(g, carry):
            step = b * n_groups + g
            slot = step % ring_slots
            for c in kv_copies(b, g, slot):
                c.wait()
            ahead = step + lookahead

            @pl.when(ahead < n_steps)
            def _():
                for c in kv_copies(lax.shift_right_logical(ahead, group_shift),
                                   ahead & (n_groups - 1), ahead % ring_slots):
                    c.start()

            k_t = jnp.concatenate([kbuf[slot, i] for i in range(pps)], axis=1).astype(BF16)
            v_t = jnp.concatenate([vbuf[slot, i] for i in range(pps)], axis=1).astype(BF16)
            s = jnp.dot(q_bd, k_t, preferred_element_type=F32)
            bias = jnp.concatenate([cpbuf[g * pps + i] for i in range(pps)], axis=1)
            s = s + jnp.concatenate([cn_q] * pps, axis=1) + jnp.concatenate([bias] * n_new, axis=0)
            return attend(carry, s, v_nt=v_t)

        m, l, acc = lax.fori_loop(0, n_groups, per_group, carry)
        out = jnp.where(head_mask, acc / l, 0.0).reshape(n_new, N_HEADS, ATTN_WIDTH)
        o_ref[tok, :] = jnp.sum(out, axis=1)

    for u in range(seq_per_step):
        per_sequence(grid_step * seq_per_step + u)

    _lru_prompt_tile(grid_step % tiles_per_batch == 0, xl_ref, szl_ref, sgb_ref, cw_ref, cb_ref,
                     wri_ref, br_ref, bi_ref, lam_ref, wpl_ref, r_ref, hlast_ref,
                     xprev_ref, hprev_ref, a_ref, b_ref, h_ref)


def _fox_sample_lru_prompt(pt_t, q, k, v, lf_t, cache_kt, cache_vt, cache_lt,
                           xl, szl, sgb, cw, cb, wri, br, bi, lam, wpl,
                           pages_per_step, ring_slots, tl):
    n_pages, n_seq = pt_t.shape
    b, t, c = xl.shape
    tiles_per_batch = t // tl
    n_tiles = b * tiles_per_batch
    assert n_seq % n_tiles == 0, "sequences must split evenly over the LRU row tiles"
    hbm = pl.BlockSpec(memory_space=pl.ANY)
    row_spec = pl.BlockSpec((1, tl, c), lambda i: (i // tiles_per_batch, i % tiles_per_batch, 0))
    sample_consts = (q, k, v, lf_t)
    lru_consts = (cw, cb, wri, br, bi, lam, wpl)
    return pl.pallas_call(
        functools.partial(_fox_sample_lru_prompt_kernel, pages_per_step=pages_per_step,
                          ring_slots=ring_slots, seq_per_step=n_seq // n_tiles,
                          tiles_per_batch=tiles_per_batch),
        grid=(n_tiles,),
        in_specs=([pl.BlockSpec(memory_space=pltpu.SMEM)]
                  + [_const_spec(x.shape) for x in sample_consts] + [hbm, hbm, hbm]
                  + [row_spec, row_spec, row_spec] + [_const_spec(x.shape) for x in lru_consts]),
        out_specs=(pl.BlockSpec(q.shape, lambda i: (0, 0)), row_spec,
                   pl.BlockSpec((1, V7X_SUBLANES, c), lambda i: (i // tiles_per_batch, 0, 0))),
        out_shape=(jax.ShapeDtypeStruct(q.shape, F32),
                   jax.ShapeDtypeStruct((b, t, D_MODEL), F32),
                   jax.ShapeDtypeStruct((b, V7X_SUBLANES, c), F32)),
        scratch_shapes=[
            pltpu.VMEM((ring_slots, pages_per_step, ATTN_WIDTH, PAGE_SIZE), F32),
            pltpu.VMEM((ring_slots, pages_per_step, ATTN_WIDTH, PAGE_SIZE), F32),
            pltpu.VMEM((2, n_pages, N_HEADS, PAGE_SIZE), F32),
            pltpu.VMEM((n_pages, N_HEADS, PAGE_SIZE), F32),
            pltpu.VMEM((n_pages * N_HEADS, n_pages * N_HEADS), BF16),
            pltpu.SemaphoreType.DMA((ring_slots,)), pltpu.SemaphoreType.DMA((ring_slots,)),
            pltpu.SemaphoreType.DMA((2,)),
            pltpu.VMEM((V7X_SUBLANES, c), F32), pltpu.VMEM((V7X_SUBLANES, c), F32),
            pltpu.VMEM((tl, c), F32), pltpu.VMEM((tl, c), F32), pltpu.VMEM((tl, c), F32),
        ],
        compiler_params=_compiler_params(("arbitrary",)),
        name="fox_sample_lru_prompt",
    )(pt_t, q, k, v, lf_t, cache_kt, cache_vt, cache_lt, xl, szl, sgb, *lru_consts)


INPROJ_ROWS = 512
LRU_ROWS = 256
ATTN_Q_ROWS = 1024
ATTN_SUB_BLOCKS = 4
MERGE_ROWS = 512
SAMPLE_ROWS = 256
SAMPLE_PAGES_PER_STEP = 8
SAMPLE_RING_SLOTS = 5

IN_SPLIT_NAMES = ("q", "k", "v", "za", "f", "xl", "zl", "ga", "gb")
IN_SPLIT_SIZES = (ATTN_WIDTH, ATTN_WIDTH, ATTN_WIDTH, ATTN_WIDTH, N_HEADS,
                  LRU_WIDTH, LRU_WIDTH, D_MODEL, D_MODEL)


def kernel(x_prompt, x_sample, cache_k, cache_v, cache_logf, state_conv, state_h, page_table, norm_gain, w_in, q_norm_gain, k_norm_gain, b_forget, conv_w, conv_b, w_rec_gate, b_rec_gate, w_in_gate, b_in_gate, lru_lambda, w_proj_attn, w_proj_lru, w_out):
    assert w_in.shape[0] == 1, "single-layer step"
    b, t, d = x_prompt.shape
    db, s, _ = x_sample.shape
    n_pool = cache_k.shape[1]

    wt = jnp.transpose(w_in[0]).astype(BF16)
    w, off = {}, 0
    for name, size in zip(IN_SPLIT_NAMES, IN_SPLIT_SIZES):
        w[name] = wt[off:off + size]
        off += size
    wf_pad = jnp.pad(w["f"], ((0, V7X_LANES - N_HEADS), (0, 0)))
    ng = norm_gain[0][None, :]
    qg, kg, bfg = q_norm_gain[0], k_norm_gain[0], b_forget[0]
    qg_row, kg_row = jnp.tile(qg, N_HEADS)[None, :], jnp.tile(kg, N_HEADS)[None, :]
    bf_row = jnp.pad(bfg, (0, V7X_LANES - N_HEADS))[None, :]
    head_of = jnp.arange(ATTN_WIDTH) // HEAD_DIM
    bd = (head_of[:, None] == head_of[None, :]).astype(BF16)
    wri = jnp.concatenate([w_rec_gate[0], w_in_gate[0]], axis=-1).astype(BF16)
    lru_consts = (conv_w[0], conv_b[0][None, :], wri, b_rec_gate[0][None, :],
                  b_in_gate[0][None, :], lru_lambda[0][None, :], w_proj_lru[0].astype(BF16))
    wpa, wout = w_proj_attn[0].astype(BF16), w_out[0].astype(BF16)

    score_bound = (LOG2_E * ATTN_SCALE * HEAD_DIM) * jnp.max(jnp.abs(qg)) * jnp.max(jnp.abs(kg))
    shift_is_safe = score_bound <= MAX_SAFE_SHIFT
    shift_col = jnp.broadcast_to(jnp.where(shift_is_safe, score_bound, 0.0), (N_HEADS, 1))
    qt, kp, kt, vt, vtb, szat, lft, xl, szl, sga, sgb = _inproj_prompt(
        x_prompt, ng, w, qg[:, None], kg[:, None], bfg[:, None], shift_col, INPROJ_ROWS)
    ot = lax.cond(shift_is_safe,
                  lambda: _fox_prompt(qt, kp, vtb, ATTN_Q_ROWS, ATTN_SUB_BLOCKS, True),
                  lambda: _fox_prompt(qt, kp, vtb, ATTN_Q_ROWS, ATTN_SUB_BLOCKS, False))
    to_tokens = lambda a: jnp.transpose(a.reshape(b, N_HEADS, HEAD_DIM, t), (0, 3, 1, 2))[None]
    k_prompt, v_prompt = to_tokens(kt), to_tokens(vt)
    logf_prompt = jnp.transpose(lft, (0, 2, 1))[None]
    conv_prompt = xl[:, t - (CONV_WIDTH - 1):, :][None]

    xs = x_sample.reshape(db * s, d)
    q_s, k_s, v_s, sza_s, lf_s, xl_s, szl_s, sga_s, sgb_s = _inproj_sample(
        xs, ng, w, wf_pad, qg_row, kg_row, bf_row, bd, SAMPLE_ROWS)
    time_major = lambda a: jnp.transpose(a.reshape(db, s, -1), (1, 0, 2))
    conv0 = jnp.transpose(state_conv[0], (1, 0, 2))
    r_tm, hlast_s = _lru_sample(time_major(xl_s), conv0, state_h[0], time_major(szl_s),
                                time_major(sgb_s), *lru_consts)
    r_s = jnp.transpose(r_tm, (1, 0, 2)).reshape(1, db * s, d)
    lf_bsh = lf_s[:, :N_HEADS].reshape(db, s, N_HEADS)
    lf_t = jnp.pad(jnp.transpose(lf_bsh, (0, 2, 1)), ((0, 0), (0, 0), (0, V7X_LANES - s)))
    cache_kt = jnp.transpose(cache_k[0], (0, 2, 3, 1)).reshape(n_pool, ATTN_WIDTH, PAGE_SIZE)
    cache_vt = jnp.transpose(cache_v[0], (0, 2, 3, 1)).reshape(n_pool, ATTN_WIDTH, PAGE_SIZE)
    cache_lt = jnp.transpose(cache_logf[0], (0, 2, 1))
    o_s, r_p, hlast_p = _fox_sample_lru_prompt(
        jnp.transpose(page_table), q_s, k_s, v_s, lf_t,
        cache_kt, cache_vt, cache_lt, xl, szl, sgb, *lru_consts,
        SAMPLE_PAGES_PER_STEP, SAMPLE_RING_SLOTS, LRU_ROWS)
    y_prompt = _merge(x_prompt, ot, szat, sga, r_p, wpa, wout, MERGE_ROWS, True)
    h_prompt = hlast_p[:, 0, :][None]
    y_sample = _merge(xs[None], o_s[None], sza_s[None], sga_s[None], r_s, wpa, wout,
                      SAMPLE_ROWS, False).reshape(db, s, d)
    k_sample = k_s.reshape(1, db, s, N_HEADS, HEAD_DIM)
    v_sample = v_s.reshape(1, db, s, N_HEADS, HEAD_DIM)
    logf_sample = lf_bsh[None]
    conv_sample = xl_s.reshape(db, s, LRU_WIDTH)[:, s - (CONV_WIDTH - 1):, :][None]
    h_sample = hlast_s[None]

    return (y_prompt, y_sample, k_prompt, v_prompt, logf_prompt, conv_prompt, h_prompt,
            k_sample, v_sample, logf_sample, conv_sample, h_sample)
```

```python
import functools

import jax
import jax.numpy as jnp
from jax import lax
from jax.experimental import pallas as pl
from jax.experimental.pallas import tpu as pltpu

F32 = jnp.float32
BF16 = jnp.bfloat16

D_MODEL = 1024
N_HEADS = 8
HEAD_DIM = 64
ATTN_WIDTH = N_HEADS * HEAD_DIM
LRU_WIDTH = D_MODEL
LRU_BLOCKS = 8
LRU_BLOCK = LRU_WIDTH // LRU_BLOCKS
CONV_WIDTH = 4
LRU_C = 8.0
NORM_EPS = 1e-6
ATTN_SCALE = HEAD_DIM ** -0.5
PAGE_SIZE = 128

V7X_LANES = 128
V7X_SUBLANES = 8
V7X_VMEM_LIMIT_BYTES = 56 * 1024 * 1024

HEAD_SLOT = 2 * HEAD_DIM
AUG_ONES = 3
V_SLOT = HEAD_DIM + 16
LOG2_E = 1.4426950408889634
MAX_SAFE_SHIFT = 50.0

NT_DIMS = (((1,), (1,)), ((), ()))


def _compiler_params(semantics):
    return pltpu.CompilerParams(dimension_semantics=semantics,
                                vmem_limit_bytes=V7X_VMEM_LIMIT_BYTES)


def _const_spec(shape):
    zeros = (0,) * len(shape)
    return pl.BlockSpec(shape, lambda *_: zeros, pipeline_mode=pl.Buffered(1))


def _split3(x):
    hi = x.astype(BF16).astype(F32)
    r = x - hi
    mid = r.astype(BF16).astype(F32)
    lo = (r - mid).astype(BF16).astype(F32)
    return hi, mid, lo


def _exact_dot(x, w):
    m = x.shape[0]
    parts = jnp.concatenate(_split3(x), axis=0).astype(BF16)
    y = jnp.dot(parts, w, preferred_element_type=F32)
    return y[:m] + y[m:2 * m] + y[2 * m:]


def _log_sigmoid(x):
    return jnp.minimum(x, 0.0) - jnp.log1p(jnp.exp(-jnp.abs(x)))


def _sigmoid(x):
    return jax.nn.sigmoid(x)


def _silu(x):
    return x * jax.nn.sigmoid(x)


def _rms_rows(x, gain_row):
    ms = jnp.mean(x * x, axis=-1, keepdims=True)
    return x * lax.rsqrt(ms + NORM_EPS) * gain_row


def _head_rms_rowmajor(u, gain_row, blockdiag_ones):
    ssq = jnp.dot((u * u).astype(BF16), blockdiag_ones, preferred_element_type=F32)
    return u * lax.rsqrt(ssq * (1.0 / HEAD_DIM) + NORM_EPS) * gain_row


def _inproj_prompt_kernel(x_ref, ng_ref, wq_ref, wk_ref, wv_ref, wza_ref, wf_ref,
                          wxl_ref, wzl_ref, wga_ref, wgb_ref,
                          qg_ref, kg_ref, bf_ref, shift_ref, tri_ref,
                          qt_ref, kp_ref, kt_ref, vt_ref, vtb_ref, szat_ref, lft_ref,
                          xl_ref, szl_ref, sga_ref, sgb_ref,
                          carry_ref):
    tm = x_ref.shape[1]

    @pl.when(pl.program_id(1) == 0)
    def _():
        carry_ref[...] = jnp.zeros_like(carry_ref)

    h = _rms_rows(x_ref[0], ng_ref[...]).astype(BF16)

    q_t = lax.dot_general(wq_ref[...], h, NT_DIMS, preferred_element_type=F32)
    k_t = lax.dot_general(wk_ref[...], h, NT_DIMS, preferred_element_type=F32)
    v_t = lax.dot_general(wv_ref[...], h, NT_DIMS, preferred_element_type=F32)
    za_t = lax.dot_general(wza_ref[...], h, NT_DIMS, preferred_element_type=F32)
    f_t = lax.dot_general(wf_ref[...], h, NT_DIMS, preferred_element_type=F32)

    logf_t = _log_sigmoid(f_t + bf_ref[...])
    lft_ref[0] = logf_t
    c_t = _exact_dot(logf_t, tri_ref[...]) + carry_ref[:, 0:1]
    carry_ref[...] = jnp.broadcast_to(c_t[:, tm - 1:tm], carry_ref.shape)
    c2_t = c_t * LOG2_E
    c_hi, c_mid, c_lo = _split3(c2_t)
    q_hi, q_mid, q_lo = _split3(c2_t - shift_ref[...])

    vt_ref[0] = v_t
    szat_ref[0] = _silu(za_t).astype(BF16)
    ones16 = jnp.ones((V_SLOT - HEAD_DIM, tm), F32)

    row8 = lax.broadcasted_iota(jnp.int32, (V7X_SUBLANES, tm), 0)
    ones_rows = jnp.where(row8 < AUG_ONES, 1.0, 0.0).astype(F32)
    zeros32 = jnp.zeros((HEAD_DIM - 4 * V7X_SUBLANES, tm), F32)
    kn_heads = []
    for hd in range(N_HEADS):
        rows = slice(hd * HEAD_DIM, (hd + 1) * HEAD_DIM)
        qh = q_t[rows]
        qn = qh * lax.rsqrt(jnp.mean(qh * qh, axis=0, keepdims=True) + NORM_EPS) * qg_ref[...]
        qn = qn * (ATTN_SCALE * LOG2_E)
        kh = k_t[rows]
        kn = kh * lax.rsqrt(jnp.mean(kh * kh, axis=0, keepdims=True) + NORM_EPS) * kg_ref[...]
        kt_ref[0, rows, :] = kn
        kn_heads.append(kn)
        vtb_ref[0, hd] = jnp.concatenate([v_t[rows], ones16], axis=0).astype(BF16)
        sel = jnp.where(row8 == hd, -1.0, 0.0).astype(F32)
        cq = jnp.where(row8 == 0, q_hi[hd:hd + 1],
                       jnp.where(row8 == 1, q_mid[hd:hd + 1],
                                 jnp.where(row8 == 2, q_lo[hd:hd + 1], 0.0)))
        aug = jnp.concatenate([cq, sel, sel, sel, zeros32], axis=0)
        if hd % 2 == 0:
            slot = jnp.concatenate([qn, aug], axis=0)
        else:
            slot = jnp.concatenate([aug, qn], axis=0)
        qt_ref[0, hd] = slot.astype(BF16)

    aug_k_t = jnp.concatenate([ones_rows, c_hi, c_mid, c_lo, zeros32], axis=0)
    aug_k = jnp.transpose(jnp.concatenate([aug_k_t, aug_k_t], axis=0))

    lane = lax.broadcasted_iota(jnp.int32, (tm, HEAD_SLOT), 1)
    for pair in range(N_HEADS // 2):
        kcol = jnp.transpose(jnp.concatenate(kn_heads[2 * pair:2 * pair + 2], axis=0))
        even = jnp.where(lane < HEAD_DIM, kcol, aug_k)
        odd = jnp.where(lane < HEAD_DIM, aug_k, kcol)
        kp_ref[0, :, (2 * pair) * HEAD_SLOT:(2 * pair + 1) * HEAD_SLOT] = even.astype(BF16)
        kp_ref[0, :, (2 * pair + 1) * HEAD_SLOT:(2 * pair + 2) * HEAD_SLOT] = odd.astype(BF16)

    xl_ref[0] = lax.dot_general(h, wxl_ref[...], NT_DIMS, preferred_element_type=F32)
    szl_ref[0] = _silu(lax.dot_general(h, wzl_ref[...], NT_DIMS,
                                       preferred_element_type=F32)).astype(BF16)
    sga_ref[0] = _sigmoid(lax.dot_general(h, wga_ref[...], NT_DIMS,
                                          preferred_element_type=F32)).astype(BF16)
    sgb_ref[0] = _sigmoid(lax.dot_general(h, wgb_ref[...], NT_DIMS,
                                          preferred_element_type=F32)).astype(BF16)


def _inproj_prompt(x, ng, w, qg_col, kg_col, bf_col, shift_col, tm):
    b, t, _ = x.shape
    tri = jnp.triu(jnp.ones((tm, tm), F32)).astype(BF16)
    row_spec = lambda n: pl.BlockSpec((1, tm, n), lambda i, j: (i, j, 0))
    col_spec = lambda n: pl.BlockSpec((1, n, tm), lambda i, j: (i, 0, j))
    out_shape = (
        jax.ShapeDtypeStruct((b, N_HEADS, HEAD_SLOT, t), BF16),
        jax.ShapeDtypeStruct((b, t, N_HEADS * HEAD_SLOT), BF16),
        jax.ShapeDtypeStruct((b, ATTN_WIDTH, t), F32),
        jax.ShapeDtypeStruct((b, ATTN_WIDTH, t), F32),
        jax.ShapeDtypeStruct((b, N_HEADS, V_SLOT, t), BF16),
        jax.ShapeDtypeStruct((b, ATTN_WIDTH, t), BF16),
        jax.ShapeDtypeStruct((b, N_HEADS, t), F32),
        jax.ShapeDtypeStruct((b, t, LRU_WIDTH), F32),
        jax.ShapeDtypeStruct((b, t, LRU_WIDTH), BF16),
        jax.ShapeDtypeStruct((b, t, D_MODEL), BF16),
        jax.ShapeDtypeStruct((b, t, D_MODEL), BF16),
    )
    out_specs = (
        pl.BlockSpec((1, N_HEADS, HEAD_SLOT, tm), lambda i, j: (i, 0, 0, j)),
        row_spec(N_HEADS * HEAD_SLOT),
        col_spec(ATTN_WIDTH), col_spec(ATTN_WIDTH),
        pl.BlockSpec((1, N_HEADS, V_SLOT, tm), lambda i, j: (i, 0, 0, j)),
        col_spec(ATTN_WIDTH), col_spec(N_HEADS),
        row_spec(LRU_WIDTH), row_spec(LRU_WIDTH), row_spec(D_MODEL), row_spec(D_MODEL),
    )
    consts = (ng, w["q"], w["k"], w["v"], w["za"], w["f"], w["xl"], w["zl"], w["ga"], w["gb"],
              qg_col, kg_col, bf_col, shift_col, tri)
    return pl.pallas_call(
        _inproj_prompt_kernel,
        grid=(b, t // tm),
        in_specs=[row_spec(D_MODEL)] + [_const_spec(c.shape) for c in consts],
        out_specs=out_specs,
        out_shape=out_shape,
        scratch_shapes=[pltpu.VMEM((N_HEADS, V7X_LANES), F32)],
        compiler_params=_compiler_params(("arbitrary", "arbitrary")),
        name="inproj_prompt",
    )(x, *consts)


def _lru_gates(xc, wri_ref, br_row, bi_row, log_sig_lam_row):
    xb = xc.astype(BF16)
    pre_r, pre_i = [], []
    for g in range(LRU_BLOCKS):
        cols = slice(g * LRU_BLOCK, (g + 1) * LRU_BLOCK)
        ri = jnp.dot(xb[:, cols], wri_ref[g], preferred_element_type=F32)
        pre_r.append(ri[:, :LRU_BLOCK])
        pre_i.append(ri[:, LRU_BLOCK:])
    r = _sigmoid(jnp.concatenate(pre_r, axis=1) + br_row)
    i = _sigmoid(jnp.concatenate(pre_i, axis=1) + bi_row)
    log_a = LRU_C * r * log_sig_lam_row
    a = jnp.exp(log_a)
    b = jnp.sqrt(-jnp.tanh(log_a) * (1.0 + a * a)) * (i * xc)
    return a, b


def _lru_prompt_tile(first_tile, xl_ref, szl_ref, sgb_ref, cw_ref, cb_ref, wri_ref, br_ref, bi_ref,
                     lam_ref, wpl_ref,
                     r_ref, hlast_ref,
                     xprev_ref, hprev_ref, a_ref, b_ref, h_ref):
    tl = xl_ref.shape[1]
    nblk = tl // V7X_SUBLANES

    @pl.when(first_tile)
    def _():
        xprev_ref[...] = jnp.zeros_like(xprev_ref)
        hprev_ref[...] = jnp.zeros_like(hprev_ref)

    x = xl_ref[0]
    prev = xprev_ref[...]
    row8 = lax.broadcasted_iota(jnp.int32, (V7X_SUBLANES, LRU_WIDTH), 0)
    xc = cb_ref[...] + x * cw_ref[CONV_WIDTH - 1:CONV_WIDTH, :]
    for back in range(1, CONV_WIDTH):
        w_row = cw_ref[CONV_WIDTH - 1 - back:CONV_WIDTH - back, :]
        rolled = pltpu.roll(x, back, 0)
        head = jnp.where(row8 < back, pltpu.roll(prev, back, 0), rolled[:V7X_SUBLANES])
        shifted = jnp.concatenate([head, rolled[V7X_SUBLANES:]], axis=0)
        xc = xc + shifted * w_row
    xprev_ref[...] = x[tl - V7X_SUBLANES:]

    a, b = _lru_gates(xc, wri_ref, br_ref[...], bi_ref[...], _log_sigmoid(lam_ref[...]))
    a_ref[...] = a
    b_ref[...] = b

    def block(k, hprev):
        rows = pl.ds(pl.multiple_of(k * V7X_SUBLANES, V7X_SUBLANES), V7X_SUBLANES)
        ak = a_ref[rows, :]
        bk = b_ref[rows, :]
        for s in (1, 2, 4):
            keep = row8 >= s
            b_sh = pltpu.roll(bk, s, 0)
            a_sh = pltpu.roll(ak, s, 0)
            bk = jnp.where(keep, ak * b_sh + bk, bk)
            ak = jnp.where(keep, ak * a_sh, ak)
        hk = ak * hprev + bk
        h_ref[rows, :] = hk
        return jnp.broadcast_to(hk[V7X_SUBLANES - 1:], hk.shape)

    hlast = lax.fori_loop(0, nblk, block, hprev_ref[...])
    hprev_ref[...] = hlast
    hlast_ref[0] = hlast

    g = (h_ref[...] * szl_ref[0].astype(F32)).astype(BF16)
    r_ref[0] = sgb_ref[0].astype(F32) * jnp.dot(g, wpl_ref[...], preferred_element_type=F32)


def _lru_sample_kernel(xl_ref, conv0_ref, h0_ref, szl_ref, sgb_ref, cw_ref, cb_ref, wri_ref,
                       br_ref, bi_ref, lam_ref, wpl_ref,
                       r_ref, hlast_ref):
    s, n, c = xl_ref.shape
    slabs = [conv0_ref[j] for j in range(CONV_WIDTH - 1)] + [xl_ref[t] for t in range(s)]
    xc = []
    for t in range(s):
        acc = cb_ref[...] + slabs[t] * cw_ref[0:1, :]
        for j in range(1, CONV_WIDTH):
            acc = acc + slabs[t + j] * cw_ref[j:j + 1, :]
        xc.append(acc)
    xc = jnp.concatenate(xc, axis=0)
    a, b = _lru_gates(xc, wri_ref, br_ref[...], bi_ref[...], _log_sigmoid(lam_ref[...]))
    h = h0_ref[...]
    hs = []
    for t in range(s):
        h = a[t * n:(t + 1) * n] * h + b[t * n:(t + 1) * n]
        hs.append(h)
    hlast_ref[...] = h
    g = (jnp.concatenate(hs, axis=0) * szl_ref[...].reshape(s * n, c).astype(F32)).astype(BF16)
    r = jnp.dot(g, wpl_ref[...], preferred_element_type=F32)
    r_ref[...] = (sgb_ref[...].reshape(s * n, c).astype(F32) * r).reshape(s, n, c)


def _lru_sample(xl, conv0, h0, szl, sgb, cw, cb, wri, br, bi, lam, wpl):
    s, n, c = xl.shape
    return pl.pallas_call(
        _lru_sample_kernel,
        out_shape=(jax.ShapeDtypeStruct((s, n, D_MODEL), F32),
                   jax.ShapeDtypeStruct((n, c), F32)),
        compiler_params=pltpu.CompilerParams(vmem_limit_bytes=V7X_VMEM_LIMIT_BYTES),
        name="lru_sample",
    )(xl, conv0, h0, szl, sgb, cw, cb, wri, br, bi, lam, wpl)


def _fox_prompt_shifted_kernel(qt_ref, kp_ref, vt_ref, ot_ref, p_ref, *, n_sub):
    tq = qt_ref.shape[3]
    ts = tq // n_sub
    qi = pl.program_id(2)
    q_t = qt_ref[0, 0]
    key_pos = lax.broadcasted_iota(jnp.int32, (ts, tq), 0)
    qry_pos = lax.broadcasted_iota(jnp.int32, (ts, tq), 1)

    def probs(k0, slot, masked):
        for u in range(n_sub):
            ku = pl.multiple_of(k0 + u * ts, ts)
            s = jnp.dot(kp_ref[0, pl.ds(ku, ts), :], q_t, preferred_element_type=F32)
            if masked:
                s = jnp.where(key_pos + u * ts <= qry_pos, s, -jnp.inf)
            p_ref[slot, u * ts:(u + 1) * ts, :] = jnp.exp2(s).astype(BF16)

    def weighted(k0, slot):
        out = None
        for u in range(n_sub):
            ku = pl.multiple_of(k0 + u * ts, ts)
            pv = jnp.dot(vt_ref[0, 0, :, pl.ds(ku, ts)], p_ref[slot, u * ts:(u + 1) * ts, :],
                         preferred_element_type=F32)
            out = pv if out is None else out + pv
        return out

    diag0 = pl.multiple_of(qi * tq, tq)
    probs(diag0, 0, True)

    def body(j, carry):
        acc, prev = carry
        pv = weighted(prev, j % 2)
        k0 = pl.multiple_of(j * tq, tq)
        probs(k0, (j + 1) % 2, False)
        return acc + pv, k0

    acc, prev = lax.fori_loop(0, qi, body, (jnp.zeros((V_SLOT, tq), F32), diag0))
    acc = acc + weighted(prev, qi % 2)
    ot_ref[0] = acc[:HEAD_DIM] / acc[HEAD_DIM:HEAD_DIM + 1]


def _fox_prompt_online_kernel(qt_ref, kp_ref, vt_ref, ot_ref):
    tq = qt_ref.shape[3]
    qi = pl.program_id(2)
    q_t = qt_ref[0, 0]

    def step(k0, carry, mask):
        m, l, acc = carry
        s = jnp.dot(kp_ref[0, pl.ds(k0, tq), :], q_t, preferred_element_type=F32)
        if mask is not None:
            s = jnp.where(mask, s, -jnp.inf)
        m_new = jnp.maximum(m, jnp.max(s, axis=0, keepdims=True))
        alpha = jnp.exp2(m - m_new)
        p = jnp.exp2(s - m_new)
        l = alpha * l + jnp.sum(p, axis=0, keepdims=True)
        v_blk = vt_ref[0, 0, 0:HEAD_DIM, pl.ds(k0, tq)]
        acc = alpha * acc + jnp.dot(v_blk, p.astype(BF16), preferred_element_type=F32)
        return m_new, l, acc

    init = (jnp.full((1, tq), -jnp.inf, F32), jnp.zeros((1, tq), F32),
            jnp.zeros((HEAD_DIM, tq), F32))
    carry = lax.fori_loop(0, qi, lambda j, c: step(pl.multiple_of(j * tq, tq), c, None), init)
    key_pos = lax.broadcasted_iota(jnp.int32, (tq, tq), 0)
    qry_pos = lax.broadcasted_iota(jnp.int32, (tq, tq), 1)
    m, l, acc = step(pl.multiple_of(qi * tq, tq), carry, key_pos <= qry_pos)
    ot_ref[0] = acc / l


def _fox_prompt(qt, kp, vtb, tq, n_sub, shifted):
    b, nh, _, t = qt.shape
    if shifted:
        body = functools.partial(_fox_prompt_shifted_kernel, n_sub=n_sub)
        scratch = [pltpu.VMEM((2, tq, tq), BF16)]
    else:
        body, scratch = _fox_prompt_online_kernel, []
    return pl.pallas_call(
        body,
        grid=(b, nh, t // tq),
        in_specs=[pl.BlockSpec((1, 1, HEAD_SLOT, tq), lambda i, h, j: (i, h, 0, j)),
                  pl.BlockSpec((1, t, HEAD_SLOT), lambda i, h, j: (i, 0, h)),
                  pl.BlockSpec((1, 1, V_SLOT, t), lambda i, h, j: (i, h, 0, 0))],
        out_specs=pl.BlockSpec((1, HEAD_DIM, tq), lambda i, h, j: (i, h, j)),
        out_shape=jax.ShapeDtypeStruct((b, ATTN_WIDTH, t), F32),
        scratch_shapes=scratch,
        compiler_params=_compiler_params(("arbitrary", "arbitrary", "arbitrary")),
        name="fox_prompt_shifted" if shifted else "fox_prompt_online",
    )(qt, kp, vtb)


def _merge_kernel(x_ref, o_ref, sza_ref, sga_ref, r_ref, wpa_ref, wout_ref, y_ref, *,
                  attn_feature_major):
    g = o_ref[0] * sza_ref[0].astype(F32)
    if attn_feature_major:
        g = jnp.transpose(g)
    a = jnp.dot(g.astype(BF16), wpa_ref[...], preferred_element_type=F32)
    mixed = sga_ref[0].astype(F32) * a + r_ref[0]
    y_ref[0] = x_ref[0] + jnp.dot(mixed.astype(BF16), wout_ref[...], preferred_element_type=F32)


def _merge(x, o, sza, sga, r, wpa, wout, tm, attn_feature_major):
    b, t, _ = x.shape
    row_spec = lambda n: pl.BlockSpec((1, tm, n), lambda i, j: (i, j, 0))
    if attn_feature_major:
        attn_spec = pl.BlockSpec((1, ATTN_WIDTH, tm), lambda i, j: (i, 0, j))
    else:
        attn_spec = row_spec(ATTN_WIDTH)
    return pl.pallas_call(
        functools.partial(_merge_kernel, attn_feature_major=attn_feature_major),
        grid=(b, t // tm),
        in_specs=[row_spec(D_MODEL), attn_spec, attn_spec, row_spec(D_MODEL), row_spec(D_MODEL),
                  _const_spec(wpa.shape), _const_spec(wout.shape)],
        out_specs=row_spec(D_MODEL),
        out_shape=jax.ShapeDtypeStruct((b, t, D_MODEL), F32),
        compiler_params=_compiler_params(("arbitrary", "arbitrary")),
        name="merge_fm" if attn_feature_major else "merge_rm",
    )(x, o, sza, sga, r, wpa, wout)


def _inproj_sample_kernel(x_ref, ng_ref, wq_ref, wk_ref, wv_ref, wza_ref, wf_ref,
                          wxl_ref, wzl_ref, wga_ref, wgb_ref,
                          qgrow_ref, kgrow_ref, bfrow_ref, bd_ref,
                          q_ref, k_ref, v_ref, sza_ref, lf_ref, xl_ref, szl_ref, sga_ref, sgb_ref):
    h = _rms_rows(x_ref[...], ng_ref[...]).astype(BF16)
    proj = lambda w_ref: lax.dot_general(h, w_ref[...], NT_DIMS, preferred_element_type=F32)
    q_ref[...] = _head_rms_rowmajor(proj(wq_ref), qgrow_ref[...], bd_ref[...]) * ATTN_SCALE
    k_ref[...] = _head_rms_rowmajor(proj(wk_ref), kgrow_ref[...], bd_ref[...])
    v_ref[...] = proj(wv_ref)
    sza_ref[...] = _silu(proj(wza_ref)).astype(BF16)
    lf_ref[...] = _log_sigmoid(proj(wf_ref) + bfrow_ref[...])
    xl_ref[...] = proj(wxl_ref)
    szl_ref[...] = _silu(proj(wzl_ref)).astype(BF16)
    sga_ref[...] = _sigmoid(proj(wga_ref)).astype(BF16)
    sgb_ref[...] = _sigmoid(proj(wgb_ref)).astype(BF16)


def _inproj_sample(x, ng, w, wf_pad, qg_row, kg_row, bf_row, bd, tm):
    n, _ = x.shape
    row_spec = lambda c: pl.BlockSpec((tm, c), lambda i: (i, 0))
    consts = (ng, w["q"], w["k"], w["v"], w["za"], wf_pad, w["xl"], w["zl"], w["ga"], w["gb"],
              qg_row, kg_row, bf_row, bd)
    widths = (ATTN_WIDTH, ATTN_WIDTH, ATTN_WIDTH, ATTN_WIDTH, V7X_LANES,
              LRU_WIDTH, LRU_WIDTH, D_MODEL, D_MODEL)
    dtypes = (F32, F32, F32, BF16, F32, F32, BF16, BF16, BF16)
    return pl.pallas_call(
        _inproj_sample_kernel,
        grid=(n // tm,),
        in_specs=[row_spec(D_MODEL)] + [_const_spec(c.shape) for c in consts],
        out_specs=tuple(row_spec(c) for c in widths),
        out_shape=tuple(jax.ShapeDtypeStruct((n, c), d) for c, d in zip(widths, dtypes)),
        compiler_params=_compiler_params(("arbitrary",)),
        name="inproj_sample",
    )(x, *consts)


def _fox_sample_lru_prompt_kernel(pt_ref, q_ref, k_ref, v_ref, lft_ref, ck_hbm, cv_hbm, cl_hbm,
                                  xl_ref, szl_ref, sgb_ref, cw_ref, cb_ref, wri_ref, br_ref, bi_ref,
                                  lam_ref, wpl_ref,
                                  o_ref, r_ref, hlast_ref,
                                  kbuf, vbuf, lbuf, cpbuf, pre_ref, ksem, vsem, lsem,
                                  xprev_ref, hprev_ref, a_ref, b_ref, h_ref, *,
                                  pages_per_step, ring_slots, seq_per_step, tiles_per_batch):
    n_pages, n_seq = pt_ref.shape
    grid_step = pl.program_id(0)
    n_new = q_ref.shape[0] // n_seq
    pps = pages_per_step
    n_groups = n_pages // pps
    assert n_groups & (n_groups - 1) == 0, "page groups per sequence must be a power of two"
    group_shift = n_groups.bit_length() - 1
    n_steps = n_seq * n_groups
    lookahead = ring_slots - 1
    rows = n_new * N_HEADS
    page_rows = n_pages * N_HEADS

    def k_copies(b, g, slot):
        return [pltpu.make_async_copy(ck_hbm.at[pt_ref[g * pps + i, b]], kbuf.at[slot, i], ksem.at[slot])
                for i in range(pps)]

    def v_copies(b, g, slot):
        return [pltpu.make_async_copy(cv_hbm.at[pt_ref[g * pps + i, b]], vbuf.at[slot, i], vsem.at[slot])
                for i in range(pps)]

    def kv_copies(b, g, slot):
        return k_copies(b, g, slot) + v_copies(b, g, slot)

    def lf_copy(b, slot, p):
        return pltpu.make_async_copy(cl_hbm.at[pt_ref[p, b]], lbuf.at[slot, p], lsem.at[slot])

    def lf_start(b, slot):
        lax.fori_loop(0, n_pages, lambda p, c: (lf_copy(b, slot, p).start(), c)[1], 0)

    def lf_wait(b, slot):
        lax.fori_loop(0, n_pages, lambda p, c: (lf_copy(b, slot, p).wait(), c)[1], 0)

    row_head = lax.broadcasted_iota(jnp.int32, (rows, ATTN_WIDTH), 0) % N_HEADS
    lane_head = lax.broadcasted_iota(jnp.int32, (rows, ATTN_WIDTH), 1) // HEAD_DIM
    head_mask = row_head == lane_head
    lane = lax.broadcasted_iota(jnp.int32, (N_HEADS, V7X_LANES), 1)
    tri = (lax.broadcasted_iota(jnp.int32, (PAGE_SIZE, PAGE_SIZE), 0)
           <= lax.broadcasted_iota(jnp.int32, (PAGE_SIZE, PAGE_SIZE), 1)).astype(BF16)

    assert lookahead <= n_groups

    @pl.when(grid_step == 0)
    def _():
        pr = lax.broadcasted_iota(jnp.int32, (page_rows, page_rows), 0)
        pc = lax.broadcasted_iota(jnp.int32, (page_rows, page_rows), 1)
        pre_ref[...] = jnp.where(pr % N_HEADS == pc % N_HEADS,
                                 jnp.where(pc // N_HEADS < pr // N_HEADS, 1.0, 0.0), 0.0).astype(BF16)
        lf_start(0, 0)
        for g0 in range(lookahead):
            for c in kv_copies(0, g0, g0):
                c.start()

    def per_sequence(b):
        lslot = b % 2
        lf_wait(b, lslot)

        @pl.when(b + 1 < n_seq)
        def _():
            lf_start(b + 1, 1 - lslot)

        lf = lbuf[lslot].reshape(page_rows, PAGE_SIZE)
        within = _exact_dot(lf, tri)
        page_tot = jnp.broadcast_to(within[:, PAGE_SIZE - 1:], within.shape)
        parts = jnp.concatenate(_split3(page_tot), axis=1).astype(BF16)
        before = jnp.dot(pre_ref[...], parts, preferred_element_type=F32)
        cp = (within + before[:, :PAGE_SIZE] + before[:, PAGE_SIZE:2 * PAGE_SIZE]
              + before[:, 2 * PAGE_SIZE:])
        total = jnp.broadcast_to(cp[page_rows - N_HEADS:, PAGE_SIZE - 1:], (N_HEADS, PAGE_SIZE))
        after = jnp.concatenate([total] * n_pages, axis=0) - cp
        cpbuf[...] = after.reshape(n_pages, N_HEADS, PAGE_SIZE)

        tok = pl.ds(pl.multiple_of(b * n_new, n_new), n_new)
        cn = lft_ref[b]
        for s in (1, 2, 4):
            cn = cn + jnp.where(lane >= s, pltpu.roll(cn, s, 1), 0.0)
        q_b = q_ref[tok, :]
        q_bd = jnp.concatenate(
            [jnp.broadcast_to(q_b[i:i + 1], (N_HEADS, ATTN_WIDTH)) for i in range(n_new)], axis=0)
        q_bd = jnp.where(head_mask, q_bd, 0.0).astype(BF16)
        cn_q = jnp.concatenate(
            [jnp.broadcast_to(cn[:, i:i + 1], (N_HEADS, V7X_LANES)) for i in range(n_new)], axis=0)

        def attend(carry, s, v_nt=None, v_nn=None):
            m, l, acc = carry
            m_new = jnp.maximum(m, jnp.max(s, axis=1, keepdims=True))
            alpha = jnp.exp(m - m_new)
            p = jnp.exp(s - m_new)
            l = alpha * l + jnp.sum(p, axis=1, keepdims=True)
            if v_nt is not None:
                pv = lax.dot_general(p.astype(BF16), v_nt, NT_DIMS, preferred_element_type=F32)
            else:
                pv = jnp.dot(p.astype(BF16), v_nn, preferred_element_type=F32)
            return m_new, l, alpha * acc + pv

        pad = jnp.zeros((V7X_LANES - n_new, ATTN_WIDTH), F32)
        k_new = jnp.concatenate([k_ref[tok, :], pad], axis=0).astype(BF16)
        v_new = jnp.concatenate([v_ref[tok, :], pad], axis=0).astype(BF16)
        s_new = lax.dot_general(q_bd, k_new, NT_DIMS, preferred_element_type=F32)
        cn_k = jnp.concatenate([cn] * n_new, axis=0)
        key_idx = lax.broadcasted_iota(jnp.int32, (rows, V7X_LANES), 1)
        qry_idx = lax.broadcasted_iota(jnp.int32, (rows, V7X_LANES), 0) // N_HEADS
        s_new = jnp.where(key_idx <= qry_idx, s_new + cn_q - cn_k, -jnp.inf)
        init = (jnp.full((rows, 1), -jnp.inf, F32), jnp.zeros((rows, 1), F32),
                jnp.zeros((rows, ATTN_WIDTH), F32))
        carry = attend(init, s_new, v_nn=v_new)

        def ring_slot(g):
            return (b * n_groups + g) % ring_slots

        def wait_keys(g):
            for c in k_copies(b, g, ring_slot(g)):
                c.wait()

        def wait_values_and_refill(g):
            for c in v_copies(b, g, ring_slot(g)):
                c.wait()
            ahead = b * n_groups + g + lookahead

            @pl.when(ahead < n_steps)
            def _():
                for c in kv_copies(lax.shift_right_logical(ahead, group_shift),
                                   ahead & (n_groups - 1), ahead % ring_slots):
                    c.start()

        def scores(g):
            slot = ring_slot(g)
            k_t = jnp.concatenate([kbuf[slot, i] for i in range(pps)], axis=1).astype(BF16)
            s = jnp.dot(q_bd, k_t, preferred_element_type=F32)
            bias = jnp.concatenate([cpbuf[g * pps + i] for i in range(pps)], axis=1)
            return s + jnp.concatenate([cn_q] * pps, axis=1) + jnp.concatenate([bias] * n_new, axis=0)

        def weigh(g, carry, s):
            slot = ring_slot(g)
            v_t = jnp.concatenate([vbuf[slot, i] for i in range(pps)], axis=1).astype(BF16)
            return attend(carry, s, v_nt=v_t)

        def per_group(g, state):
            carry, s = state
            wait_keys(g + 1)
            wait_values_and_refill(g)
            s_next = scores(g + 1)
            return weigh(g, carry, s), s_next

        wait_keys(0)
        carry, s_last = lax.fori_loop(0, n_groups - 1, per_group, (carry, scores(0)))
        wait_values_and_refill(n_groups - 1)
        m, l, acc = weigh(n_groups - 1, carry, s_last)
        out = jnp.where(head_mask, acc / l, 0.0).reshape(n_new, N_HEADS, ATTN_WIDTH)
        o_ref[tok, :] = jnp.sum(out, axis=1)

    for u in range(seq_per_step):
        per_sequence(grid_step * seq_per_step + u)

    _lru_prompt_tile(grid_step % tiles_per_batch == 0, xl_ref, szl_ref, sgb_ref, cw_ref, cb_ref,
                     wri_ref, br_ref, bi_ref, lam_ref, wpl_ref, r_ref, hlast_ref,
                     xprev_ref, hprev_ref, a_ref, b_ref, h_ref)


def _fox_sample_lru_prompt(pt_t, q, k, v, lf_t, cache_kt, cache_vt, cache_lt,
                           xl, szl, sgb, cw, cb, wri, br, bi, lam, wpl,
                           pages_per_step, ring_slots, tl):
    n_pages, n_seq = pt_t.shape
    b, t, c = xl.shape
    tiles_per_batch = t // tl
    n_tiles = b * tiles_per_batch
    assert n_seq % n_tiles == 0, "sequences must split evenly over the LRU row tiles"
    hbm = pl.BlockSpec(memory_space=pl.ANY)
    row_spec = pl.BlockSpec((1, tl, c), lambda i: (i // tiles_per_batch, i % tiles_per_batch, 0))
    sample_consts = (q, k, v, lf_t)
    lru_consts = (cw, cb, wri, br, bi, lam, wpl)
    return pl.pallas_call(
        functools.partial(_fox_sample_lru_prompt_kernel, pages_per_step=pages_per_step,
                          ring_slots=ring_slots, seq_per_step=n_seq // n_tiles,
                          tiles_per_batch=tiles_per_batch),
        grid=(n_tiles,),
        in_specs=([pl.BlockSpec(memory_space=pltpu.SMEM)]
                  + [_const_spec(x.shape) for x in sample_consts] + [hbm, hbm, hbm]
                  + [row_spec, row_spec, row_spec] + [_const_spec(x.shape) for x in lru_consts]),
        out_specs=(pl.BlockSpec(q.shape, lambda i: (0, 0)), row_spec,
                   pl.BlockSpec((1, V7X_SUBLANES, c), lambda i: (i // tiles_per_batch, 0, 0))),
        out_shape=(jax.ShapeDtypeStruct(q.shape, F32),
                   jax.ShapeDtypeStruct((b, t, D_MODEL), F32),
                   jax.ShapeDtypeStruct((b, V7X_SUBLANES, c), F32)),
        scratch_shapes=[
            pltpu.VMEM((ring_slots, pages_per_step, ATTN_WIDTH, PAGE_SIZE), F32),
            pltpu.VMEM((ring_slots, pages_per_step, ATTN_WIDTH, PAGE_SIZE), F32),
            pltpu.VMEM((2, n_pages, N_HEADS, PAGE_SIZE), F32),
            pltpu.VMEM((n_pages, N_HEADS, PAGE_SIZE), F32),
            pltpu.VMEM((n_pages * N_HEADS, n_pages * N_HEADS), BF16),
            pltpu.SemaphoreType.DMA((ring_slots,)), pltpu.SemaphoreType.DMA((ring_slots,)),
            pltpu.SemaphoreType.DMA((2,)),
            pltpu.VMEM((V7X_SUBLANES, c), F32), pltpu.VMEM((V7X_SUBLANES, c), F32),
            pltpu.VMEM((tl, c), F32), pltpu.VMEM((tl, c), F32), pltpu.VMEM((tl, c), F32),
        ],
        compiler_params=_compiler_params(("arbitrary",)),
        name="fox_sample_lru_prompt",
    )(pt_t, q, k, v, lf_t, cache_kt, cache_vt, cache_lt, xl, szl, sgb, *lru_consts)


INPROJ_ROWS = 512
LRU_ROWS = 256
ATTN_Q_ROWS = 1024
ATTN_SUB_BLOCKS = 4
MERGE_ROWS = 512
SAMPLE_ROWS = 256
SAMPLE_PAGES_PER_STEP = 8
SAMPLE_RING_SLOTS = 5

IN_SPLIT_NAMES = ("q", "k", "v", "za", "f", "xl", "zl", "ga", "gb")
IN_SPLIT_SIZES = (ATTN_WIDTH, ATTN_WIDTH, ATTN_WIDTH, ATTN_WIDTH, N_HEADS,
                  LRU_WIDTH, LRU_WIDTH, D_MODEL, D_MODEL)


def kernel(x_prompt, x_sample, cache_k, cache_v, cache_logf, state_conv, state_h, page_table, norm_gain, w_in, q_norm_gain, k_norm_gain, b_forget, conv_w, conv_b, w_rec_gate, b_rec_gate, w_in_gate, b_in_gate, lru_lambda, w_proj_attn, w_proj_lru, w_out):
    assert w_in.shape[0] == 1, "single-layer step"
    b, t, d = x_prompt.shape
    db, s, _ = x_sample.shape
    n_pool = cache_k.shape[1]

    wt = jnp.transpose(w_in[0]).astype(BF16)
    w, off = {}, 0
    for name, size in zip(IN_SPLIT_NAMES, IN_SPLIT_SIZES):
        w[name] = wt[off:off + size]
        off += size
    wf_pad = jnp.pad(w["f"], ((0, V7X_LANES - N_HEADS), (0, 0)))
    ng = norm_gain[0][None, :]
    qg, kg, bfg = q_norm_gain[0], k_norm_gain[0], b_forget[0]
    qg_row, kg_row = jnp.tile(qg, N_HEADS)[None, :], jnp.tile(kg, N_HEADS)[None, :]
    bf_row = jnp.pad(bfg, (0, V7X_LANES - N_HEADS))[None, :]
    head_of = jnp.arange(ATTN_WIDTH) // HEAD_DIM
    bd = (head_of[:, None] == head_of[None, :]).astype(BF16)
    wri = jnp.concatenate([w_rec_gate[0], w_in_gate[0]], axis=-1).astype(BF16)
    lru_consts = (conv_w[0], conv_b[0][None, :], wri, b_rec_gate[0][None, :],
                  b_in_gate[0][None, :], lru_lambda[0][None, :], w_proj_lru[0].astype(BF16))
    wpa, wout = w_proj_attn[0].astype(BF16), w_out[0].astype(BF16)

    score_bound = (LOG2_E * ATTN_SCALE * HEAD_DIM) * jnp.max(jnp.abs(qg)) * jnp.max(jnp.abs(kg))
    shift_is_safe = score_bound <= MAX_SAFE_SHIFT
    shift_col = jnp.broadcast_to(jnp.where(shift_is_safe, score_bound, 0.0), (N_HEADS, 1))
    qt, kp, kt, vt, vtb, szat, lft, xl, szl, sga, sgb = _inproj_prompt(
        x_prompt, ng, w, qg[:, None], kg[:, None], bfg[:, None], shift_col, INPROJ_ROWS)
    ot = lax.cond(shift_is_safe,
                  lambda: _fox_prompt(qt, kp, vtb, ATTN_Q_ROWS, ATTN_SUB_BLOCKS, True),
                  lambda: _fox_prompt(qt, kp, vtb, ATTN_Q_ROWS, ATTN_SUB_BLOCKS, False))
    to_tokens = lambda a: jnp.transpose(a.reshape(b, N_HEADS, HEAD_DIM, t), (0, 3, 1, 2))[None]
    k_prompt, v_prompt = to_tokens(kt), to_tokens(vt)
    logf_prompt = jnp.transpose(lft, (0, 2, 1))[None]
    conv_prompt = xl[:, t - (CONV_WIDTH - 1):, :][None]

    xs = x_sample.reshape(db * s, d)
    q_s, k_s, v_s, sza_s, lf_s, xl_s, szl_s, sga_s, sgb_s = _inproj_sample(
        xs, ng, w, wf_pad, qg_row, kg_row, bf_row, bd, SAMPLE_ROWS)
    time_major = lambda a: jnp.transpose(a.reshape(db, s, -1), (1, 0, 2))
    conv0 = jnp.transpose(state_conv[0], (1, 0, 2))
    r_tm, hlast_s = _lru_sample(time_major(xl_s), conv0, state_h[0], time_major(szl_s),
                                time_major(sgb_s), *lru_consts)
    r_s = jnp.transpose(r_tm, (1, 0, 2)).reshape(1, db * s, d)
    lf_bsh = lf_s[:, :N_HEADS].reshape(db, s, N_HEADS)
    lf_t = jnp.pad(jnp.transpose(lf_bsh, (0, 2, 1)), ((0, 0), (0, 0), (0, V7X_LANES - s)))
    cache_kt = jnp.transpose(cache_k[0], (0, 2, 3, 1)).reshape(n_pool, ATTN_WIDTH, PAGE_SIZE)
    cache_vt = jnp.transpose(cache_v[0], (0, 2, 3, 1)).reshape(n_pool, ATTN_WIDTH, PAGE_SIZE)
    cache_lt = jnp.transpose(cache_logf[0], (0, 2, 1))
    o_s, r_p, hlast_p = _fox_sample_lru_prompt(
        jnp.transpose(page_table), q_s, k_s, v_s, lf_t,
        cache_kt, cache_vt, cache_lt, xl, szl, sgb, *lru_consts,
        SAMPLE_PAGES_PER_STEP, SAMPLE_RING_SLOTS, LRU_ROWS)
    y_prompt = _merge(x_prompt, ot, szat, sga, r_p, wpa, wout, MERGE_ROWS, True)
    h_prompt = hlast_p[:, 0, :][None]
    y_sample = _merge(xs[None], o_s[None], sza_s[None], sga_s[None], r_s, wpa, wout,
                      SAMPLE_ROWS, False).reshape(db, s, d)
    k_sample = k_s.reshape(1, db, s, N_HEADS, HEAD_DIM)
    v_sample = v_s.reshape(1, db, s, N_HEADS, HEAD_DIM)
    logf_sample = lf_bsh[None]
    conv_sample = xl_s.reshape(db, s, LRU_WIDTH)[:, s - (CONV_WIDTH - 1):, :][None]
    h_sample = hlast_s[None]

    return (y_prompt, y_sample, k_prompt, v_prompt, logf_prompt, conv_prompt, h_prompt,
            k_sample, v_sample, logf_sample, conv_sample, h_sample)
```

```python
import functools

import jax
import jax.numpy as jnp
from jax import lax
from jax.experimental import pallas as pl
from jax.experimental.pallas import tpu as pltpu

F32 = jnp.float32
BF16 = jnp.bfloat16

D_MODEL = 1024
N_HEADS = 8
HEAD_DIM = 64
ATTN_WIDTH = N_HEADS * HEAD_DIM
LRU_WIDTH = D_MODEL
LRU_BLOCKS = 8
LRU_BLOCK = LRU_WIDTH // LRU_BLOCKS
CONV_WIDTH = 4
LRU_C = 8.0
NORM_EPS = 1e-6
ATTN_SCALE = HEAD_DIM ** -0.5
PAGE_SIZE = 128

V7X_LANES = 128
V7X_SUBLANES = 8
V7X_VMEM_LIMIT_BYTES = 56 * 1024 * 1024

HEAD_SLOT = 2 * HEAD_DIM
AUG_ONES = 3
V_SLOT = HEAD_DIM + 16
LOG2_E = 1.4426950408889634
MAX_SAFE_SHIFT = 50.0

NT_DIMS = (((1,), (1,)), ((), ()))


def _compiler_params(semantics):
    return pltpu.CompilerParams(dimension_semantics=semantics,
                                vmem_limit_bytes=V7X_VMEM_LIMIT_BYTES)


def _const_spec(shape):
    zeros = (0,) * len(shape)
    return pl.BlockSpec(shape, lambda *_: zeros, pipeline_mode=pl.Buffered(1))


def _split3(x):
    hi = x.astype(BF16).astype(F32)
    r = x - hi
    mid = r.astype(BF16).astype(F32)
    lo = (r - mid).astype(BF16).astype(F32)
    return hi, mid, lo


def _exact_dot(x, w):
    m = x.shape[0]
    parts = jnp.concatenate(_split3(x), axis=0).astype(BF16)
    y = jnp.dot(parts, w, preferred_element_type=F32)
    return y[:m] + y[m:2 * m] + y[2 * m:]


def _log_sigmoid(x):
    return jnp.minimum(x, 0.0) - jnp.log1p(jnp.exp(-jnp.abs(x)))


def _sigmoid(x):
    return jax.nn.sigmoid(x)


def _silu(x):
    return x * jax.nn.sigmoid(x)


def _rms_rows(x, gain_row):
    ms = jnp.mean(x * x, axis=-1, keepdims=True)
    return x * lax.rsqrt(ms + NORM_EPS) * gain_row


def _head_rms_rowmajor(u, gain_row, blockdiag_ones):
    ssq = jnp.dot((u * u).astype(BF16), blockdiag_ones, preferred_element_type=F32)
    return u * lax.rsqrt(ssq * (1.0 / HEAD_DIM) + NORM_EPS) * gain_row


def _inproj_prompt_kernel(x_ref, ng_ref, wq_ref, wk_ref, wv_ref, wza_ref, wf_ref,
                          wxl_ref, wzl_ref, wga_ref, wgb_ref,
                          qg_ref, kg_ref, bf_ref, shift_ref, tri_ref,
                          qt_ref, kp_ref, kt_ref, vt_ref, vtb_ref, szat_ref, lft_ref,
                          xl_ref, szl_ref, sga_ref, sgb_ref,
                          carry_ref):
    tm = x_ref.shape[1]

    @pl.when(pl.program_id(1) == 0)
    def _():
        carry_ref[...] = jnp.zeros_like(carry_ref)

    h = _rms_rows(x_ref[0], ng_ref[...]).astype(BF16)

    q_t = lax.dot_general(wq_ref[...], h, NT_DIMS, preferred_element_type=F32)
    k_t = lax.dot_general(wk_ref[...], h, NT_DIMS, preferred_element_type=F32)
    v_t = lax.dot_general(wv_ref[...], h, NT_DIMS, preferred_element_type=F32)
    za_t = lax.dot_general(wza_ref[...], h, NT_DIMS, preferred_element_type=F32)
    f_t = lax.dot_general(wf_ref[...], h, NT_DIMS, preferred_element_type=F32)

    logf_t = _log_sigmoid(f_t + bf_ref[...])
    lft_ref[0] = logf_t
    c_t = _exact_dot(logf_t, tri_ref[...]) + carry_ref[:, 0:1]
    carry_ref[...] = jnp.broadcast_to(c_t[:, tm - 1:tm], carry_ref.shape)
    c2_t = c_t * LOG2_E
    c_hi, c_mid, c_lo = _split3(c2_t)
    q_hi, q_mid, q_lo = _split3(c2_t - shift_ref[...])

    vt_ref[0] = v_t
    szat_ref[0] = _silu(za_t).astype(BF16)
    ones16 = jnp.ones((V_SLOT - HEAD_DIM, tm), F32)

    row8 = lax.broadcasted_iota(jnp.int32, (V7X_SUBLANES, tm), 0)
    ones_rows = jnp.where(row8 < AUG_ONES, 1.0, 0.0).astype(F32)
    zeros32 = jnp.zeros((HEAD_DIM - 4 * V7X_SUBLANES, tm), F32)
    kn_heads = []
    for hd in range(N_HEADS):
        rows = slice(hd * HEAD_DIM, (hd + 1) * HEAD_DIM)
        qh = q_t[rows]
        qn = qh * lax.rsqrt(jnp.mean(qh * qh, axis=0, keepdims=True) + NORM_EPS) * qg_ref[...]
        qn = qn * (ATTN_SCALE * LOG2_E)
        kh = k_t[rows]
        kn = kh * lax.rsqrt(jnp.mean(kh * kh, axis=0, keepdims=True) + NORM_EPS) * kg_ref[...]
        kt_ref[0, rows, :] = kn
        kn_heads.append(kn)
        vtb_ref[0, hd] = jnp.concatenate([v_t[rows], ones16], axis=0).astype(BF16)
        sel = jnp.where(row8 == hd, -1.0, 0.0).astype(F32)
        cq = jnp.where(row8 == 0, q_hi[hd:hd + 1],
                       jnp.where(row8 == 1, q_mid[hd:hd + 1],
                                 jnp.where(row8 == 2, q_lo[hd:hd + 1], 0.0)))
        aug = jnp.concatenate([cq, sel, sel, sel, zeros32], axis=0)
        if hd % 2 == 0:
            slot = jnp.concatenate([qn, aug], axis=0)
        else:
            slot = jnp.concatenate([aug, qn], axis=0)
        qt_ref[0, hd] = slot.astype(BF16)

    aug_k_t = jnp.concatenate([ones_rows, c_hi, c_mid, c_lo, zeros32], axis=0)
    aug_k = jnp.transpose(jnp.concatenate([aug_k_t, aug_k_t], axis=0))

    lane = lax.broadcasted_iota(jnp.int32, (tm, HEAD_SLOT), 1)
    for pair in range(N_HEADS // 2):
        kcol = jnp.transpose(jnp.concatenate(kn_heads[2 * pair:2 * pair + 2], axis=0))
        even = jnp.where(lane < HEAD_DIM, kcol, aug_k)
        odd = jnp.where(lane < HEAD_DIM, aug_k, kcol)
        kp_ref[0, :, (2 * pair) * HEAD_SLOT:(2 * pair + 1) * HEAD_SLOT] = even.astype(BF16)
        kp_ref[0, :, (2 * pair + 1) * HEAD_SLOT:(2 * pair + 2) * HEAD_SLOT] = odd.astype(BF16)

    xl_ref[0] = lax.dot_general(h, wxl_ref[...], NT_DIMS, preferred_element_type=F32)
    szl_ref[0] = _silu(lax.dot_general(h, wzl_ref[...], NT_DIMS,
                                       preferred_element_type=F32)).astype(BF16)
    sga_ref[0] = _sigmoid(lax.dot_general(h, wga_ref[...], NT_DIMS,
                                          preferred_element_type=F32)).astype(BF16)
    sgb_ref[0] = _sigmoid(lax.dot_general(h, wgb_ref[...], NT_DIMS,
                                          preferred_element_type=F32)).astype(BF16)


def _inproj_prompt(x, ng, w, qg_col, kg_col, bf_col, shift_col, tm):
    b, t, _ = x.shape
    tri = jnp.triu(jnp.ones((tm, tm), F32)).astype(BF16)
    row_spec = lambda n: pl.BlockSpec((1, tm, n), lambda i, j: (i, j, 0))
    col_spec = lambda n: pl.BlockSpec((1, n, tm), lambda i, j: (i, 0, j))
    out_shape = (
        jax.ShapeDtypeStruct((b, N_HEADS, HEAD_SLOT, t), BF16),
        jax.ShapeDtypeStruct((b, t, N_HEADS * HEAD_SLOT), BF16),
        jax.ShapeDtypeStruct((b, ATTN_WIDTH, t), F32),
        jax.ShapeDtypeStruct((b, ATTN_WIDTH, t), F32),
        jax.ShapeDtypeStruct((b, N_HEADS, V_SLOT, t), BF16),
        jax.ShapeDtypeStruct((b, ATTN_WIDTH, t), BF16),
        jax.ShapeDtypeStruct((b, N_HEADS, t), F32),
        jax.ShapeDtypeStruct((b, t, LRU_WIDTH), F32),
        jax.ShapeDtypeStruct((b, t, LRU_WIDTH), BF16),
        jax.ShapeDtypeStruct((b, t, D_MODEL), BF16),
        jax.ShapeDtypeStruct((b, t, D_MODEL), BF16),
    )
    out_specs = (
        pl.BlockSpec((1, N_HEADS, HEAD_SLOT, tm), lambda i, j: (i, 0, 0, j)),
        row_spec(N_HEADS * HEAD_SLOT),
        col_spec(ATTN_WIDTH), col_spec(ATTN_WIDTH),
        pl.BlockSpec((1, N_HEADS, V_SLOT, tm), lambda i, j: (i, 0, 0, j)),
        col_spec(ATTN_WIDTH), col_spec(N_HEADS),
        row_spec(LRU_WIDTH), row_spec(LRU_WIDTH), row_spec(D_MODEL), row_spec(D_MODEL),
    )
    consts = (ng, w["q"], w["k"], w["v"], w["za"], w["f"], w["xl"], w["zl"], w["ga"], w["gb"],
              qg_col, kg_col, bf_col, shift_col, tri)
    return pl.pallas_call(
        _inproj_prompt_kernel,
        grid=(b, t // tm),
        in_specs=[row_spec(D_MODEL)] + [_const_spec(c.shape) for c in consts],
        out_specs=out_specs,
        out_shape=out_shape,
        scratch_shapes=[pltpu.VMEM((N_HEADS, V7X_LANES), F32)],
        compiler_params=_compiler_params(("arbitrary", "arbitrary")),
        name="inproj_prompt",
    )(x, *consts)


def _lru_gates(xc, wri_ref, br_row, bi_row, log_sig_lam_row):
    xb = xc.astype(BF16)
    pre_r, pre_i = [], []
    for g in range(LRU_BLOCKS):
        cols = slice(g * LRU_BLOCK, (g + 1) * LRU_BLOCK)
        ri = jnp.dot(xb[:, cols], wri_ref[g], preferred_element_type=F32)
        pre_r.append(ri[:, :LRU_BLOCK])
        pre_i.append(ri[:, LRU_BLOCK:])
    r = _sigmoid(jnp.concatenate(pre_r, axis=1) + br_row)
    i = _sigmoid(jnp.concatenate(pre_i, axis=1) + bi_row)
    log_a = LRU_C * r * log_sig_lam_row
    a = jnp.exp(log_a)
    b = jnp.sqrt(-jnp.tanh(log_a) * (1.0 + a * a)) * (i * xc)
    return a, b


def _lru_prompt_tile(first_tile, xl_ref, szl_ref, sgb_ref, cw_ref, cb_ref, wri_ref, br_ref, bi_ref,
                     lam_ref, wpl_ref,
                     r_ref, hlast_ref,
                     xprev_ref, hprev_ref, a_ref, b_ref, h_ref):
    tl = xl_ref.shape[1]
    nblk = tl // V7X_SUBLANES

    @pl.when(first_tile)
    def _():
        xprev_ref[...] = jnp.zeros_like(xprev_ref)
        hprev_ref[...] = jnp.zeros_like(hprev_ref)

    x = xl_ref[0]
    prev = xprev_ref[...]
    row8 = lax.broadcasted_iota(jnp.int32, (V7X_SUBLANES, LRU_WIDTH), 0)
    xc = cb_ref[...] + x * cw_ref[CONV_WIDTH - 1:CONV_WIDTH, :]
    for back in range(1, CONV_WIDTH):
        w_row = cw_ref[CONV_WIDTH - 1 - back:CONV_WIDTH - back, :]
        rolled = pltpu.roll(x, back, 0)
        head = jnp.where(row8 < back, pltpu.roll(prev, back, 0), rolled[:V7X_SUBLANES])
        shifted = jnp.concatenate([head, rolled[V7X_SUBLANES:]], axis=0)
        xc = xc + shifted * w_row
    xprev_ref[...] = x[tl - V7X_SUBLANES:]

    a, b = _lru_gates(xc, wri_ref, br_ref[...], bi_ref[...], _log_sigmoid(lam_ref[...]))
    a_ref[...] = a
    b_ref[...] = b

    def block(k, hprev):
        rows = pl.ds(pl.multiple_of(k * V7X_SUBLANES, V7X_SUBLANES), V7X_SUBLANES)
        ak = a_ref[rows, :]
        bk = b_ref[rows, :]
        for s in (1, 2, 4):
            keep = row8 >= s
            b_sh = pltpu.roll(bk, s, 0)
            a_sh = pltpu.roll(ak, s, 0)
            bk = jnp.where(keep, ak * b_sh + bk, bk)
            ak = jnp.where(keep, ak * a_sh, ak)
        hk = ak * hprev + bk
        h_ref[rows, :] = hk
        return jnp.broadcast_to(hk[V7X_SUBLANES - 1:], hk.shape)

    hlast = lax.fori_loop(0, nblk, block, hprev_ref[...])
    hprev_ref[...] = hlast
    hlast_ref[0] = hlast

    g = (h_ref[...] * szl_ref[0].astype(F32)).astype(BF16)
    r_ref[0] = sgb_ref[0].astype(F32) * jnp.dot(g, wpl_ref[...], preferred_element_type=F32)


def _lru_sample_kernel(xl_ref, conv0_ref, h0_ref, szl_ref, sgb_ref, cw_ref, cb_ref, wri_ref,
                       br_ref, bi_ref, lam_ref, wpl_ref,
                       r_ref, hlast_ref):
    s, n, c = xl_ref.shape
    slabs = [conv0_ref[j] for j in range(CONV_WIDTH - 1)] + [xl_ref[t] for t in range(s)]
    xc = []
    for t in range(s):
        acc = cb_ref[...] + slabs[t] * cw_ref[0:1, :]
        for j in range(1, CONV_WIDTH):
            acc = acc + slabs[t + j] * cw_ref[j:j + 1, :]
        xc.append(acc)
    xc = jnp.concatenate(xc, axis=0)
    a, b = _lru_gates(xc, wri_ref, br_ref[...], bi_ref[...], _log_sigmoid(lam_ref[...]))
    h = h0_ref[...]
    hs = []
    for t in range(s):
        h = a[t * n:(t + 1) * n] * h + b[t * n:(t + 1) * n]
        hs.append(h)
    hlast_ref[...] = h
    g = (jnp.concatenate(hs, axis=0) * szl_ref[...].reshape(s * n, c).astype(F32)).astype(BF16)
    r = jnp.dot(g, wpl_ref[...], preferred_element_type=F32)
    r_ref[...] = (sgb_ref[...].reshape(s * n, c).astype(F32) * r).reshape(s, n, c)


def _lru_sample(xl, conv0, h0, szl, sgb, cw, cb, wri, br, bi, lam, wpl):
    s, n, c = xl.shape
    return pl.pallas_call(
        _lru_sample_kernel,
        out_shape=(jax.ShapeDtypeStruct((s, n, D_MODEL), F32),
                   jax.ShapeDtypeStruct((n, c), F32)),
        compiler_params=pltpu.CompilerParams(vmem_limit_bytes=V7X_VMEM_LIMIT_BYTES),
        name="lru_sample",
    )(xl, conv0, h0, szl, sgb, cw, cb, wri, br, bi, lam, wpl)


def _fox_prompt_shifted_kernel(qt_ref, kp_ref, vt_ref, ot_ref, p_ref, *, n_sub):
    tq = qt_ref.shape[3]
    ts = tq // n_sub
    qi = pl.program_id(2)
    q_t = qt_ref[0, 0]
    key_pos = lax.broadcasted_iota(jnp.int32, (ts, tq), 0)
    qry_pos = lax.broadcasted_iota(jnp.int32, (ts, tq), 1)

    def probs(k0, slot, masked):
        for u in range(n_sub):
            ku = pl.multiple_of(k0 + u * ts, ts)
            s = jnp.dot(kp_ref[0, pl.ds(ku, ts), :], q_t, preferred_element_type=F32)
            if masked:
                s = jnp.where(key_pos + u * ts <= qry_pos, s, -jnp.inf)
            p_ref[slot, u * ts:(u + 1) * ts, :] = jnp.exp2(s).astype(BF16)

    def weighted(k0, slot):
        out = None
        for u in range(n_sub):
            ku = pl.multiple_of(k0 + u * ts, ts)
            pv = jnp.dot(vt_ref[0, 0, :, pl.ds(ku, ts)], p_ref[slot, u * ts:(u + 1) * ts, :],
                         preferred_element_type=F32)
            out = pv if out is None else out + pv
        return out

    diag0 = pl.multiple_of(qi * tq, tq)
    probs(diag0, 0, True)

    def body(j, carry):
        acc, prev = carry
        pv = weighted(prev, j % 2)
        k0 = pl.multiple_of(j * tq, tq)
        probs(k0, (j + 1) % 2, False)
        return acc + pv, k0

    acc, prev = lax.fori_loop(0, qi, body, (jnp.zeros((V_SLOT, tq), F32), diag0))
    acc = acc + weighted(prev, qi % 2)
    ot_ref[0] = acc[:HEAD_DIM] / acc[HEAD_DIM:HEAD_DIM + 1]


def _fox_prompt_online_kernel(qt_ref, kp_ref, vt_ref, ot_ref):
    tq = qt_ref.shape[3]
    qi = pl.program_id(2)
    q_t = qt_ref[0, 0]

    def step(k0, carry, mask):
        m, l, acc = carry
        s = jnp.dot(kp_ref[0, pl.ds(k0, tq), :], q_t, preferred_element_type=F32)
        if mask is not None:
            s = jnp.where(mask, s, -jnp.inf)
        m_new = jnp.maximum(m, jnp.max(s, axis=0, keepdims=True))
        alpha = jnp.exp2(m - m_new)
        p = jnp.exp2(s - m_new)
        l = alpha * l + jnp.sum(p, axis=0, keepdims=True)
        v_blk = vt_ref[0, 0, 0:HEAD_DIM, pl.ds(k0, tq)]
        acc = alpha * acc + jnp.dot(v_blk, p.astype(BF16), preferred_element_type=F32)
        return m_new, l, acc

    init = (jnp.full((1, tq), -jnp.inf, F32), jnp.zeros((1, tq), F32),
            jnp.zeros((HEAD_DIM, tq), F32))
    carry = lax.fori_loop(0, qi, lambda j, c: step(pl.multiple_of(j * tq, tq), c, None), init)
    key_pos = lax.broadcasted_iota(jnp.int32, (tq, tq), 0)
    qry_pos = lax.broadcasted_iota(jnp.int32, (tq, tq), 1)
    m, l, acc = step(pl.multiple_of(qi * tq, tq), carry, key_pos <= qry_pos)
    ot_ref[0] = acc / l


def _fox_prompt(qt, kp, vtb, tq, n_sub, shifted):
    b, nh, _, t = qt.shape
    if shifted:
        body = functools.partial(_fox_prompt_shifted_kernel, n_sub=n_sub)
        scratch = [pltpu.VMEM((2, tq, tq), BF16)]
    else:
        body, scratch = _fox_prompt_online_kernel, []
    return pl.pallas_call(
        body,
        grid=(b, nh, t // tq),
        in_specs=[pl.BlockSpec((1, 1, HEAD_SLOT, tq), lambda i, h, j: (i, h, 0, j)),
                  pl.BlockSpec((1, t, HEAD_SLOT), lambda i, h, j: (i, 0, h)),
                  pl.BlockSpec((1, 1, V_SLOT, t), lambda i, h, j: (i, h, 0, 0))],
        out_specs=pl.BlockSpec((1, HEAD_DIM, tq), lambda i, h, j: (i, h, j)),
        out_shape=jax.ShapeDtypeStruct((b, ATTN_WIDTH, t), F32),
        scratch_shapes=scratch,
        compiler_params=_compiler_params(("arbitrary", "arbitrary", "arbitrary")),
        name="fox_prompt_shifted" if shifted else "fox_prompt_online",
    )(qt, kp, vtb)


def _merge_kernel(x_ref, o_ref, sza_ref, sga_ref, r_ref, wpa_ref, wout_ref, y_ref, *,
                  attn_feature_major):
    g = o_ref[0] * sza_ref[0].astype(F32)
    if attn_feature_major:
        g = jnp.transpose(g)
    a = jnp.dot(g.astype(BF16), wpa_ref[...], preferred_element_type=F32)
    mixed = sga_ref[0].astype(F32) * a + r_ref[0]
    y_ref[0] = x_ref[0] + jnp.dot(mixed.astype(BF16), wout_ref[...], preferred_element_type=F32)


def _merge(x, o, sza, sga, r, wpa, wout, tm, attn_feature_major):
    b, t, _ = x.shape
    row_spec = lambda n: pl.BlockSpec((1, tm, n), lambda i, j: (i, j, 0))
    if attn_feature_major:
        attn_spec = pl.BlockSpec((1, ATTN_WIDTH, tm), lambda i, j: (i, 0, j))
    else:
        attn_spec = row_spec(ATTN_WIDTH)
    return pl.pallas_call(
        functools.partial(_merge_kernel, attn_feature_major=attn_feature_major),
        grid=(b, t // tm),
        in_specs=[row_spec(D_MODEL), attn_spec, attn_spec, row_spec(D_MODEL), row_spec(D_MODEL),
                  _const_spec(wpa.shape), _const_spec(wout.shape)],
        out_specs=row_spec(D_MODEL),
        out_shape=jax.ShapeDtypeStruct((b, t, D_MODEL), F32),
        compiler_params=_compiler_params(("arbitrary", "arbitrary")),
        name="merge_fm" if attn_feature_major else "merge_rm",
    )(x, o, sza, sga, r, wpa, wout)


def _inproj_sample_kernel(x_ref, ng_ref, wq_ref, wk_ref, wv_ref, wza_ref, wf_ref,
                          wxl_ref, wzl_ref, wga_ref, wgb_ref,
                          qgrow_ref, kgrow_ref, bfrow_ref, bd_ref,
                          q_ref, k_ref, v_ref, sza_ref, lf_ref, xl_ref, szl_ref, sga_ref, sgb_ref):
    h = _rms_rows(x_ref[...], ng_ref[...]).astype(BF16)
    proj = lambda w_ref: lax.dot_general(h, w_ref[...], NT_DIMS, preferred_element_type=F32)
    q_ref[...] = _head_rms_rowmajor(proj(wq_ref), qgrow_ref[...], bd_ref[...]) * ATTN_SCALE
    k_ref[...] = _head_rms_rowmajor(proj(wk_ref), kgrow_ref[...], bd_ref[...])
    v_ref[...] = proj(wv_ref)
    sza_ref[...] = _silu(proj(wza_ref)).astype(BF16)
    lf_ref[...] = _log_sigmoid(proj(wf_ref) + bfrow_ref[...])
    xl_ref[...] = proj(wxl_ref)
    szl_ref[...] = _silu(proj(wzl_ref)).astype(BF16)
    sga_ref[...] = _sigmoid(proj(wga_ref)).astype(BF16)
    sgb_ref[...] = _sigmoid(proj(wgb_ref)).astype(BF16)


def _inproj_sample(x, ng, w, wf_pad, qg_row, kg_row, bf_row, bd, tm):
    n, _ = x.shape
    row_spec = lambda c: pl.BlockSpec((tm, c), lambda i: (i, 0))
    consts = (ng, w["q"], w["k"], w["v"], w["za"], wf_pad, w["xl"], w["zl"], w["ga"], w["gb"],
              qg_row, kg_row, bf_row, bd)
    widths = (ATTN_WIDTH, ATTN_WIDTH, ATTN_WIDTH, ATTN_WIDTH, V7X_LANES,
              LRU_WIDTH, LRU_WIDTH, D_MODEL, D_MODEL)
    dtypes = (F32, F32, F32, BF16, F32, F32, BF16, BF16, BF16)
    return pl.pallas_call(
        _inproj_sample_kernel,
        grid=(n // tm,),
        in_specs=[row_spec(D_MODEL)] + [_const_spec(c.shape) for c in consts],
        out_specs=tuple(row_spec(c) for c in widths),
        out_shape=tuple(jax.ShapeDtypeStruct((n, c), d) for c, d in zip(widths, dtypes)),
        compiler_params=_compiler_params(("arbitrary",)),
        name="inproj_sample",
    )(x, *consts)


def _fox_sample_lru_prompt_kernel(pt_ref, q_ref, k_ref, v_ref, lft_ref, ck_hbm, cv_hbm, cl_hbm,
                                  xl_ref, szl_ref, sgb_ref, cw_ref, cb_ref, wri_ref, br_ref, bi_ref,
                                  lam_ref, wpl_ref,
                                  x_ref, ot_ref, szat_ref, sga_ref, wpa_ref, wout_ref,
                                  o_ref, y_ref, hlast_ref,
                                  kbuf, vbuf, lbuf, cpbuf, pre_ref, ksem, vsem, lsem,
                                  xprev_ref, hprev_ref, a_ref, b_ref, h_ref, r_ref, *,
                                  pages_per_step, ring_slots, seq_per_step, tiles_per_batch):
    n_pages, n_seq = pt_ref.shape
    grid_step = pl.program_id(0)
    n_new = q_ref.shape[0] // n_seq
    pps = pages_per_step
    n_groups = n_pages // pps
    assert n_groups & (n_groups - 1) == 0, "page groups per sequence must be a power of two"
    group_shift = n_groups.bit_length() - 1
    n_steps = n_seq * n_groups
    lookahead = ring_slots - 1
    rows = n_new * N_HEADS
    page_rows = n_pages * N_HEADS

    def kv_copies(b, g, slot):
        out = []
        for i in range(pps):
            page = pt_ref[g * pps + i, b]
            out.append(pltpu.make_async_copy(ck_hbm.at[page], kbuf.at[slot, i], ksem.at[slot]))
            out.append(pltpu.make_async_copy(cv_hbm.at[page], vbuf.at[slot, i], vsem.at[slot]))
        return out

    def lf_copy(b, slot, p):
        return pltpu.make_async_copy(cl_hbm.at[pt_ref[p, b]], lbuf.at[slot, p], lsem.at[slot])

    def lf_start(b, slot):
        lax.fori_loop(0, n_pages, lambda p, c: (lf_copy(b, slot, p).start(), c)[1], 0)

    def lf_wait(b, slot):
        lax.fori_loop(0, n_pages, lambda p, c: (lf_copy(b, slot, p).wait(), c)[1], 0)

    row_head = lax.broadcasted_iota(jnp.int32, (rows, ATTN_WIDTH), 0) % N_HEADS
    lane_head = lax.broadcasted_iota(jnp.int32, (rows, ATTN_WIDTH), 1) // HEAD_DIM
    head_mask = row_head == lane_head
    lane = lax.broadcasted_iota(jnp.int32, (N_HEADS, V7X_LANES), 1)
    tri = (lax.broadcasted_iota(jnp.int32, (PAGE_SIZE, PAGE_SIZE), 0)
           <= lax.broadcasted_iota(jnp.int32, (PAGE_SIZE, PAGE_SIZE), 1)).astype(BF16)

    assert lookahead <= n_groups

    @pl.when(grid_step == 0)
    def _():
        pr = lax.broadcasted_iota(jnp.int32, (page_rows, page_rows), 0)
        pc = lax.broadcasted_iota(jnp.int32, (page_rows, page_rows), 1)
        pre_ref[...] = jnp.where(pr % N_HEADS == pc % N_HEADS,
                                 jnp.where(pc // N_HEADS < pr // N_HEADS, 1.0, 0.0), 0.0).astype(BF16)
        lf_start(0, 0)
        for g0 in range(lookahead):
            for c in kv_copies(0, g0, g0):
                c.start()

    def per_sequence(b):
        lslot = b % 2
        lf_wait(b, lslot)

        @pl.when(b + 1 < n_seq)
        def _():
            lf_start(b + 1, 1 - lslot)

        lf = lbuf[lslot].reshape(page_rows, PAGE_SIZE)
        within = _exact_dot(lf, tri)
        page_tot = jnp.broadcast_to(within[:, PAGE_SIZE - 1:], within.shape)
        parts = jnp.concatenate(_split3(page_tot), axis=1).astype(BF16)
        before = jnp.dot(pre_ref[...], parts, preferred_element_type=F32)
        cp = (within + before[:, :PAGE_SIZE] + before[:, PAGE_SIZE:2 * PAGE_SIZE]
              + before[:, 2 * PAGE_SIZE:])
        total = jnp.broadcast_to(cp[page_rows - N_HEADS:, PAGE_SIZE - 1:], (N_HEADS, PAGE_SIZE))
        after = jnp.concatenate([total] * n_pages, axis=0) - cp
        cpbuf[...] = after.reshape(n_pages, N_HEADS, PAGE_SIZE)

        tok = pl.ds(pl.multiple_of(b * n_new, n_new), n_new)
        cn = lft_ref[b]
        for s in (1, 2, 4):
            cn = cn + jnp.where(lane >= s, pltpu.roll(cn, s, 1), 0.0)
        q_b = q_ref[tok, :]
        q_bd = jnp.concatenate(
            [jnp.broadcast_to(q_b[i:i + 1], (N_HEADS, ATTN_WIDTH)) for i in range(n_new)], axis=0)
        q_bd = jnp.where(head_mask, q_bd, 0.0).astype(BF16)
        cn_q = jnp.concatenate(
            [jnp.broadcast_to(cn[:, i:i + 1], (N_HEADS, V7X_LANES)) for i in range(n_new)], axis=0)

        def attend(carry, s, v_nt=None, v_nn=None):
            m, l, acc = carry
            m_new = jnp.maximum(m, jnp.max(s, axis=1, keepdims=True))
            alpha = jnp.exp(m - m_new)
            p = jnp.exp(s - m_new)
            l = alpha * l + jnp.sum(p, axis=1, keepdims=True)
            if v_nt is not None:
                pv = lax.dot_general(p.astype(BF16), v_nt, NT_DIMS, preferred_element_type=F32)
            else:
                pv = jnp.dot(p.astype(BF16), v_nn, preferred_element_type=F32)
            return m_new, l, alpha * acc + pv

        pad = jnp.zeros((V7X_LANES - n_new, ATTN_WIDTH), F32)
        k_new = jnp.concatenate([k_ref[tok, :], pad], axis=0).astype(BF16)
        v_new = jnp.concatenate([v_ref[tok, :], pad], axis=0).astype(BF16)
        s_new = lax.dot_general(q_bd, k_new, NT_DIMS, preferred_element_type=F32)
        cn_k = jnp.concatenate([cn] * n_new, axis=0)
        key_idx = lax.broadcasted_iota(jnp.int32, (rows, V7X_LANES), 1)
        qry_idx = lax.broadcasted_iota(jnp.int32, (rows, V7X_LANES), 0) // N_HEADS
        s_new = jnp.where(key_idx <= qry_idx, s_new + cn_q - cn_k, -jnp.inf)
        init = (jnp.full((rows, 1), -jnp.inf, F32), jnp.zeros((rows, 1), F32),
                jnp.zeros((rows, ATTN_WIDTH), F32))
        carry = attend(init, s_new, v_nn=v_new)

        def per_group(g, carry):
            step = b * n_groups + g
            slot = step % ring_slots
            for c in kv_copies(b, g, slot):
                c.wait()
            ahead = step + lookahead

            @pl.when(ahead < n_steps)
            def _():
                for c in kv_copies(lax.shift_right_logical(ahead, group_shift),
                                   ahead & (n_groups - 1), ahead % ring_slots):
                    c.start()

            k_t = jnp.concatenate([kbuf[slot, i] for i in range(pps)], axis=1).astype(BF16)
            v_t = jnp.concatenate([vbuf[slot, i] for i in range(pps)], axis=1).astype(BF16)
            s = jnp.dot(q_bd, k_t, preferred_element_type=F32)
            bias = jnp.concatenate([cpbuf[g * pps + i] for i in range(pps)], axis=1)
            s = s + jnp.concatenate([cn_q] * pps, axis=1) + jnp.concatenate([bias] * n_new, axis=0)
            return attend(carry, s, v_nt=v_t)

        m, l, acc = lax.fori_loop(0, n_groups, per_group, carry)
        out = jnp.where(head_mask, acc / l, 0.0).reshape(n_new, N_HEADS, ATTN_WIDTH)
        o_ref[tok, :] = jnp.sum(out, axis=1)

    for u in range(seq_per_step):
        per_sequence(grid_step * seq_per_step + u)

    _lru_prompt_tile(grid_step % tiles_per_batch == 0, xl_ref, szl_ref, sgb_ref, cw_ref, cb_ref,
                     wri_ref, br_ref, bi_ref, lam_ref, wpl_ref, r_ref, hlast_ref,
                     xprev_ref, hprev_ref, a_ref, b_ref, h_ref)
    _merge_kernel(x_ref, ot_ref, szat_ref, sga_ref, r_ref, wpa_ref, wout_ref, y_ref,
                  attn_feature_major=True)


def _fox_sample_lru_prompt(pt_t, q, k, v, lf_t, cache_kt, cache_vt, cache_lt,
                           xl, szl, sgb, cw, cb, wri, br, bi, lam, wpl,
                           x, ot, szat, sga, wpa, wout,
                           pages_per_step, ring_slots, tl):
    n_pages, n_seq = pt_t.shape
    b, t, c = xl.shape
    assert c == D_MODEL
    tiles_per_batch = t // tl
    n_tiles = b * tiles_per_batch
    assert n_seq % n_tiles == 0, "sequences must split evenly over the LRU row tiles"
    hbm = pl.BlockSpec(memory_space=pl.ANY)
    row_spec = pl.BlockSpec((1, tl, c), lambda i: (i // tiles_per_batch, i % tiles_per_batch, 0))
    col_spec = pl.BlockSpec((1, ATTN_WIDTH, tl),
                            lambda i: (i // tiles_per_batch, 0, i % tiles_per_batch))
    sample_consts = (q, k, v, lf_t)
    lru_consts = (cw, cb, wri, br, bi, lam, wpl)
    return pl.pallas_call(
        functools.partial(_fox_sample_lru_prompt_kernel, pages_per_step=pages_per_step,
                          ring_slots=ring_slots, seq_per_step=n_seq // n_tiles,
                          tiles_per_batch=tiles_per_batch),
        grid=(n_tiles,),
        in_specs=([pl.BlockSpec(memory_space=pltpu.SMEM)]
                  + [_const_spec(x.shape) for x in sample_consts] + [hbm, hbm, hbm]
                  + [row_spec, row_spec, row_spec] + [_const_spec(a.shape) for a in lru_consts]
                  + [row_spec, col_spec, col_spec, row_spec,
                     _const_spec(wpa.shape), _const_spec(wout.shape)]),
        out_specs=(pl.BlockSpec(q.shape, lambda i: (0, 0)), row_spec,
                   pl.BlockSpec((1, V7X_SUBLANES, c), lambda i: (i // tiles_per_batch, 0, 0))),
        out_shape=(jax.ShapeDtypeStruct(q.shape, F32),
                   jax.ShapeDtypeStruct((b, t, D_MODEL), F32),
                   jax.ShapeDtypeStruct((b, V7X_SUBLANES, c), F32)),
        scratch_shapes=[
            pltpu.VMEM((ring_slots, pages_per_step, ATTN_WIDTH, PAGE_SIZE), F32),
            pltpu.VMEM((ring_slots, pages_per_step, ATTN_WIDTH, PAGE_SIZE), F32),
            pltpu.VMEM((2, n_pages, N_HEADS, PAGE_SIZE), F32),
            pltpu.VMEM((n_pages, N_HEADS, PAGE_SIZE), F32),
            pltpu.VMEM((n_pages * N_HEADS, n_pages * N_HEADS), BF16),
            pltpu.SemaphoreType.DMA((ring_slots,)), pltpu.SemaphoreType.DMA((ring_slots,)),
            pltpu.SemaphoreType.DMA((2,)),
            pltpu.VMEM((V7X_SUBLANES, c), F32), pltpu.VMEM((V7X_SUBLANES, c), F32),
            pltpu.VMEM((tl, c), F32), pltpu.VMEM((tl, c), F32), pltpu.VMEM((tl, c), F32),
            pltpu.VMEM((1, tl, c), F32),
        ],
        compiler_params=_compiler_params(("arbitrary",)),
        name="fox_sample_lru_prompt",
    )(pt_t, q, k, v, lf_t, cache_kt, cache_vt, cache_lt, xl, szl, sgb, *lru_consts,
      x, ot, szat, sga, wpa, wout)


INPROJ_ROWS = 512
LRU_ROWS = 256
ATTN_Q_ROWS = 1024
ATTN_SUB_BLOCKS = 4
SAMPLE_ROWS = 256
SAMPLE_PAGES_PER_STEP = 8
SAMPLE_RING_SLOTS = 5

IN_SPLIT_NAMES = ("q", "k", "v", "za", "f", "xl", "zl", "ga", "gb")
IN_SPLIT_SIZES = (ATTN_WIDTH, ATTN_WIDTH, ATTN_WIDTH, ATTN_WIDTH, N_HEADS,
                  LRU_WIDTH, LRU_WIDTH, D_MODEL, D_MODEL)


def kernel(x_prompt, x_sample, cache_k, cache_v, cache_logf, state_conv, state_h, page_table, norm_gain, w_in, q_norm_gain, k_norm_gain, b_forget, conv_w, conv_b, w_rec_gate, b_rec_gate, w_in_gate, b_in_gate, lru_lambda, w_proj_attn, w_proj_lru, w_out):
    assert w_in.shape[0] == 1, "single-layer step"
    b, t, d = x_prompt.shape
    db, s, _ = x_sample.shape
    n_pool = cache_k.shape[1]

    wt = jnp.transpose(w_in[0]).astype(BF16)
    w, off = {}, 0
    for name, size in zip(IN_SPLIT_NAMES, IN_SPLIT_SIZES):
        w[name] = wt[off:off + size]
        off += size
    wf_pad = jnp.pad(w["f"], ((0, V7X_LANES - N_HEADS), (0, 0)))
    ng = norm_gain[0][None, :]
    qg, kg, bfg = q_norm_gain[0], k_norm_gain[0], b_forget[0]
    qg_row, kg_row = jnp.tile(qg, N_HEADS)[None, :], jnp.tile(kg, N_HEADS)[None, :]
    bf_row = jnp.pad(bfg, (0, V7X_LANES - N_HEADS))[None, :]
    head_of = jnp.arange(ATTN_WIDTH) // HEAD_DIM
    bd = (head_of[:, None] == head_of[None, :]).astype(BF16)
    wri = jnp.concatenate([w_rec_gate[0], w_in_gate[0]], axis=-1).astype(BF16)
    lru_consts = (conv_w[0], conv_b[0][None, :], wri, b_rec_gate[0][None, :],
                  b_in_gate[0][None, :], lru_lambda[0][None, :], w_proj_lru[0].astype(BF16))
    wpa, wout = w_proj_attn[0].astype(BF16), w_out[0].astype(BF16)

    score_bound = (LOG2_E * ATTN_SCALE * HEAD_DIM) * jnp.max(jnp.abs(qg)) * jnp.max(jnp.abs(kg))
    shift_is_safe = score_bound <= MAX_SAFE_SHIFT
    shift_col = jnp.broadcast_to(jnp.where(shift_is_safe, score_bound, 0.0), (N_HEADS, 1))
    qt, kp, kt, vt, vtb, szat, lft, xl, szl, sga, sgb = _inproj_prompt(
        x_prompt, ng, w, qg[:, None], kg[:, None], bfg[:, None], shift_col, INPROJ_ROWS)
    ot = lax.cond(shift_is_safe,
                  lambda: _fox_prompt(qt, kp, vtb, ATTN_Q_ROWS, ATTN_SUB_BLOCKS, True),
                  lambda: _fox_prompt(qt, kp, vtb, ATTN_Q_ROWS, ATTN_SUB_BLOCKS, False))
    to_tokens = lambda a: jnp.transpose(a.reshape(b, N_HEADS, HEAD_DIM, t), (0, 3, 1, 2))[None]
    k_prompt, v_prompt = to_tokens(kt), to_tokens(vt)
    logf_prompt = jnp.transpose(lft, (0, 2, 1))[None]
    conv_prompt = xl[:, t - (CONV_WIDTH - 1):, :][None]

    xs = x_sample.reshape(db * s, d)
    q_s, k_s, v_s, sza_s, lf_s, xl_s, szl_s, sga_s, sgb_s = _inproj_sample(
        xs, ng, w, wf_pad, qg_row, kg_row, bf_row, bd, SAMPLE_ROWS)
    time_major = lambda a: jnp.transpose(a.reshape(db, s, -1), (1, 0, 2))
    conv0 = jnp.transpose(state_conv[0], (1, 0, 2))
    r_tm, hlast_s = _lru_sample(time_major(xl_s), conv0, state_h[0], time_major(szl_s),
                                time_major(sgb_s), *lru_consts)
    r_s = jnp.transpose(r_tm, (1, 0, 2)).reshape(1, db * s, d)
    lf_bsh = lf_s[:, :N_HEADS].reshape(db, s, N_HEADS)
    lf_t = jnp.pad(jnp.transpose(lf_bsh, (0, 2, 1)), ((0, 0), (0, 0), (0, V7X_LANES - s)))
    cache_kt = jnp.transpose(cache_k[0], (0, 2, 3, 1)).reshape(n_pool, ATTN_WIDTH, PAGE_SIZE)
    cache_vt = jnp.transpose(cache_v[0], (0, 2, 3, 1)).reshape(n_pool, ATTN_WIDTH, PAGE_SIZE)
    cache_lt = jnp.transpose(cache_logf[0], (0, 2, 1))
    o_s, y_prompt, hlast_p = _fox_sample_lru_prompt(
        jnp.transpose(page_table), q_s, k_s, v_s, lf_t,
        cache_kt, cache_vt, cache_lt, xl, szl, sgb, *lru_consts,
        x_prompt, ot, szat, sga, wpa, wout,
        SAMPLE_PAGES_PER_STEP, SAMPLE_RING_SLOTS, LRU_ROWS)
    h_prompt = hlast_p[:, 0, :][None]
    y_sample = _merge(xs[None], o_s[None], sza_s[None], sga_s[None], r_s, wpa, wout,
                      SAMPLE_ROWS, False).reshape(db, s, d)
    k_sample = k_s.reshape(1, db, s, N_HEADS, HEAD_DIM)
    v_sample = v_s.reshape(1, db, s, N_HEADS, HEAD_DIM)
    logf_sample = lf_bsh[None]
    conv_sample = xl_s.reshape(db, s, LRU_WIDTH)[:, s - (CONV_WIDTH - 1):, :][None]
    h_sample = hlast_s[None]

    return (y_prompt, y_sample, k_prompt, v_prompt, logf_prompt, conv_prompt, h_prompt,
            k_sample, v_sample, logf_sample, conv_sample, h_sample)
```
